```python
import math
import jax, jax.numpy as jnp
from jax import lax
import numpy as np

D_MODEL = 4096
BATCH = 1
SEQ = 16384
DEPTH = 2

GRID_W = 64
CTX_LEN = 256
DA_HEADS = 16
DA_QK_DIM = 64
DA_V_DIM = 2 * DA_QK_DIM
DA_QK_W = DA_HEADS * 2 * DA_QK_DIM
DA_WIDTH = DA_HEADS * DA_V_DIM
ROPE_THETA = 10000.0
ROPE_FREQS = DA_QK_DIM // 4
Q_BLOCK = 128
GM_GROUPS = 16
GM_CHUNK = 128
GM_GROUP_DIM = 128
GM_WIDTH = GM_GROUPS * GM_GROUP_DIM
D_FF = 11008
N_EXPERTS = 8
TOP_K = 2
EXPERT_FF = 3072
N_DENSE = (DEPTH + 1) // 2
N_MOE = DEPTH // 2
NORM_EPS = 1e-6
IN_WIDTH = 2 * DA_QK_W + DA_WIDTH + 2 * GM_WIDTH + 2 * D_MODEL
IN_SPLITS = [DA_QK_W, 2 * DA_QK_W, 2 * DA_QK_W + DA_WIDTH, 2 * DA_QK_W + DA_WIDTH + GM_WIDTH,
             2 * DA_QK_W + DA_WIDTH + 2 * GM_WIDTH, 2 * DA_QK_W + DA_WIDTH + 2 * GM_WIDTH + D_MODEL]

kernel_name = "hybrid_diffattn_gmlp_moe_diffusion_trunk"


def rmsnorm(x, w):
    xf = x.astype(jnp.float32)
    y = xf * lax.rsqrt(jnp.mean(xf * xf, axis=-1, keepdims=True) + NORM_EPS)
    return (y * w.astype(jnp.float32)).astype(x.dtype)


def layernorm(x, w, b):
    xf = x.astype(jnp.float32)
    mu = jnp.mean(xf, axis=-1, keepdims=True)
    xc = xf - mu
    y = xc * lax.rsqrt(jnp.mean(xc * xc, axis=-1, keepdims=True) + NORM_EPS)
    return (y * w.astype(jnp.float32) + b.astype(jnp.float32)).astype(x.dtype)


def modulate(x, shift, scale):
    return x * (1 + scale) + shift


def axial_rope_tables(n_tokens):
    rows_n = n_tokens // GRID_W
    rows = jnp.repeat(jnp.arange(rows_n), GRID_W)
    cols = jnp.tile(jnp.arange(GRID_W), rows_n)
    inv = 1.0 / (ROPE_THETA ** (jnp.arange(ROPE_FREQS, dtype=jnp.float32) / ROPE_FREQS))
    ang = jnp.stack([rows.astype(jnp.float32)[:, None] * inv,
                     cols.astype(jnp.float32)[:, None] * inv], axis=1)
    return jnp.cos(ang), jnp.sin(ang)


def apply_rope(x, cos, sin):
    xs = x.astype(jnp.float32).reshape(x.shape[:-1] + (2, 2, ROPE_FREQS))
    x1, x2 = xs[..., 0, :], xs[..., 1, :]
    c = cos[:, None, None]
    s = sin[:, None, None]
    out = jnp.stack([x1 * c - x2 * s, x2 * c + x1 * s], axis=-2)
    return out.reshape(x.shape).astype(x.dtype)


def diff_attend(q, k, v, lam):
    s = jnp.einsum('bqhmd,bkhmd->bhmqk', q, k).astype(jnp.float32) * (DA_QK_DIM ** -0.5)
    p = jax.nn.softmax(s, axis=-1)
    a = p[:, :, 0] - lam * p[:, :, 1]
    return jnp.einsum('bhqk,bkhd->bqhd', a.astype(v.dtype), v)


def diff_head_norm(o, subln_w, lambda_init):
    B, T = o.shape[:2]
    return (rmsnorm(o, subln_w) * (1.0 - lambda_init)).reshape(B, T, DA_WIDTH)


def spatial_gating(u, v, gn_w, gn_b, w_s, b_s):
    v = layernorm(v, gn_w, gn_b)
    B, T, _ = v.shape
    vc = v.reshape(B, T // GM_CHUNK, GM_CHUNK, GM_GROUPS, GM_GROUP_DIM)
    mixed = jnp.einsum('gpq,bnqgc->bnpgc', w_s, vc) + b_s.T[None, None, :, :, None]
    return u * mixed.reshape(B, T, GM_WIDTH)


def gated_merge(attn_o, gm_o, ga, gb, w_pa, w_pb, w_out):
    y = jax.nn.sigmoid(ga) * (attn_o @ w_pa) + jax.nn.sigmoid(gb) * (gm_o @ w_pb)
    return y @ w_out


def token_mixers(h_lat, h_ctx, cos, sin, w_in, lq1, lk1, lq2, lk2, subln_w, gn_w, gn_b,
                 w_s, b_s, w_pa, w_pb, w_out, lambda_init, need_ctx):
    B, S, _ = h_lat.shape
    C = h_ctx.shape[1]
    lam = (jnp.exp(jnp.sum(lq1.astype(jnp.float32) * lk1.astype(jnp.float32)))
           - jnp.exp(jnp.sum(lq2.astype(jnp.float32) * lk2.astype(jnp.float32))) + lambda_init)
    ql, kl, vl, ul, gvl, gal, gbl = jnp.split(h_lat @ w_in, IN_SPLITS, axis=-1)
    ql = apply_rope(ql.reshape(B, S, DA_HEADS, 2, DA_QK_DIM), cos, sin)
    kl = apply_rope(kl.reshape(B, S, DA_HEADS, 2, DA_QK_DIM), cos, sin)
    vl = vl.reshape(B, S, DA_HEADS, DA_V_DIM)
    if need_ctx:
        qc, kc, vc, uc, gvc, gac, gbc = jnp.split(h_ctx @ w_in, IN_SPLITS, axis=-1)
    else:
        kc, vc = jnp.split(h_ctx @ w_in[:, DA_QK_W:2 * DA_QK_W + DA_WIDTH], [DA_QK_W], axis=-1)
    kc = kc.reshape(B, C, DA_HEADS, 2, DA_QK_DIM)
    vc = vc.reshape(B, C, DA_HEADS, DA_V_DIM)
    k_all = jnp.concatenate([kl, kc], axis=1)
    v_all = jnp.concatenate([vl, vc], axis=1)
    nb = S // Q_BLOCK
    qb = ql.reshape(B, nb, Q_BLOCK, DA_HEADS, 2, DA_QK_DIM).swapaxes(0, 1)
    ob = lax.map(lambda qi: diff_attend(qi, k_all, v_all, lam), qb)
    attn_l = diff_head_norm(ob.swapaxes(0, 1).reshape(B, S, DA_HEADS, DA_V_DIM), subln_w, lambda_init)
    gm_l = spatial_gating(ul, gvl, gn_w, gn_b, w_s, b_s)
    out_l = gated_merge(attn_l, gm_l, gal, gbl, w_pa, w_pb, w_out)
    if not need_ctx:
        return out_l, None
    qc = qc.reshape(B, C, DA_HEADS, 2, DA_QK_DIM)
    attn_c = diff_head_norm(diff_attend(qc, kc, vc, lam), subln_w, lambda_init)
    gm_c = spatial_gating(uc, gvc, gn_w, gn_b, w_s, b_s)
    out_c = gated_merge(attn_c, gm_c, gac, gbc, w_pa, w_pb, w_out)
    return out_l, out_c


def swiglu(h, w_gate, w_up, w_down):
    return (jax.nn.silu(h @ w_gate) * (h @ w_up)) @ w_down


def moe_swiglu(h, w_router, b_router, w_gate, w_up, w_down):
    logits = (h @ w_router + b_router).astype(jnp.float32)
    top_v, top_i = lax.top_k(logits, TOP_K)
    wts = jax.nn.softmax(top_v, axis=-1)
    combine = jnp.sum(jax.nn.one_hot(top_i, N_EXPERTS, dtype=jnp.float32) * wts[..., None], axis=-2)
    out = jnp.zeros_like(h)
    for e in range(N_EXPERTS):
        out = out + combine[..., e:e + 1].astype(h.dtype) * swiglu(h, w_gate[e], w_up[e], w_down[e])
    return out


def channel_mixer(h, l, ffn_w_gate, ffn_w_up, ffn_w_down, moe_w_router, moe_b_router,
                  moe_w_gate, moe_w_up, moe_w_down):
    i = l // 2
    if l % 2 == 0:
        return swiglu(h, ffn_w_gate[i], ffn_w_up[i], ffn_w_down[i])
    return moe_swiglu(h, moe_w_router[i], moe_b_router[i], moe_w_gate[i], moe_w_up[i], moe_w_down[i])


def setup_inputs(seed: int = 0) -> dict:
    key = jax.random.key(seed)
    ks = iter(jax.random.split(key, 40))
    f32 = jnp.float32
    D = D_MODEL
    L = DEPTH

    def nrm(shape, scale):
        return jax.random.normal(next(ks), shape, f32) * scale

    return {
        "x": nrm((BATCH, SEQ, D), 1.0),
        "c": nrm((BATCH, D), 1.0),
        "ctx": nrm((BATCH, CTX_LEN, D), 1.0),
        "c_ctx": nrm((D,), 1.0),
        "w_ada": nrm((L, D, 6 * D), D ** -0.5),
        "b_ada": nrm((L, 6 * D), 0.02),
        "pre_norm_mix": 1.0 + nrm((L, D), 0.05),
        "post_norm_mix": 1.0 + nrm((L, D), 0.05),
        "pre_norm_ffn": 1.0 + nrm((L, D), 0.05),
        "post_norm_ffn": 1.0 + nrm((L, D), 0.05),
        "w_in": nrm((L, D, IN_WIDTH), D ** -0.5),
        "lambda_q1": nrm((L, DA_QK_DIM), 0.1),
        "lambda_k1": nrm((L, DA_QK_DIM), 0.1),
        "lambda_q2": nrm((L, DA_QK_DIM), 0.1),
        "lambda_k2": nrm((L, DA_QK_DIM), 0.1),
        "da_subln": 1.0 + nrm((L, DA_V_DIM), 0.05),
        "gm_norm_w": 1.0 + nrm((L, GM_WIDTH), 0.05),
        "gm_norm_b": nrm((L, GM_WIDTH), 0.02),
        "gm_w_s": nrm((L, GM_GROUPS, GM_CHUNK, GM_CHUNK), GM_CHUNK ** -0.5),
        "gm_b_s": 1.0 + nrm((L, GM_GROUPS, GM_CHUNK), 0.02),
        "w_branch_attn": nrm((L, DA_WIDTH, D), DA_WIDTH ** -0.5),
        "w_branch_gmlp": nrm((L, GM_WIDTH, D), GM_WIDTH ** -0.5),
        "w_out": nrm((L, D, D), D ** -0.5),
        "ffn_w_gate": nrm((N_DENSE, D, D_FF), D ** -0.5),
        "ffn_w_up": nrm((N_DENSE, D, D_FF), D ** -0.5),
        "ffn_w_down": nrm((N_DENSE, D_FF, D), D_FF ** -0.5),
        "moe_w_router": nrm((N_MOE, D, N_EXPERTS), D ** -0.5),
        "moe_b_router": nrm((N_MOE, N_EXPERTS), 0.01),
        "moe_w_gate": nrm((N_MOE, N_EXPERTS, D, EXPERT_FF), D ** -0.5),
        "moe_w_up": nrm((N_MOE, N_EXPERTS, D, EXPERT_FF), D ** -0.5),
        "moe_w_down": nrm((N_MOE, N_EXPERTS, EXPERT_FF, D), EXPERT_FF ** -0.5),
    }


def reference(x, c, ctx, c_ctx, w_ada, b_ada, pre_norm_mix, post_norm_mix, pre_norm_ffn,
              post_norm_ffn, w_in, lambda_q1, lambda_k1, lambda_q2, lambda_k2, da_subln,
              gm_norm_w, gm_norm_b, gm_w_s, gm_b_s, w_branch_attn, w_branch_gmlp, w_out,
              ffn_w_gate, ffn_w_up, ffn_w_down, moe_w_router, moe_b_router, moe_w_gate,
              moe_w_up, moe_w_down):
    cos, sin = axial_rope_tables(x.shape[1])
    xl, xc = x, ctx
    silu_c = jax.nn.silu(c)
    silu_cc = jax.nn.silu(c_ctx)
    for l in range(DEPTH):
        need_ctx = l < DEPTH - 1
        lambda_init = 0.8 - 0.6 * math.exp(-0.3 * l)
        mod_l = jnp.split((silu_c @ w_ada[l] + b_ada[l])[:, None, :], 6, axis=-1)
        mod_c = jnp.split(silu_cc @ w_ada[l] + b_ada[l], 6, axis=-1)
        sh1, sc1, g1, sh2, sc2, g2 = mod_l
        csh1, csc1, cg1, csh2, csc2, cg2 = mod_c
        hl = modulate(rmsnorm(xl, pre_norm_mix[l]), sh1, sc1)
        hc = modulate(rmsnorm(xc, pre_norm_mix[l]), csh1, csc1)
        ol, oc = token_mixers(hl, hc, cos, sin, w_in[l], lambda_q1[l], lambda_k1[l], lambda_q2[l],
                              lambda_k2[l], da_subln[l], gm_norm_w[l], gm_norm_b[l], gm_w_s[l],
                              gm_b_s[l], w_branch_attn[l], w_branch_gmlp[l], w_out[l],
                              lambda_init, need_ctx)
        xl = xl + g1 * rmsnorm(ol, post_norm_mix[l])
        hl = modulate(rmsnorm(xl, pre_norm_ffn[l]), sh2, sc2)
        fl = channel_mixer(hl, l, ffn_w_gate, ffn_w_up, ffn_w_down, moe_w_router, moe_b_router,
                           moe_w_gate, moe_w_up, moe_w_down)
        xl = xl + g2 * rmsnorm(fl, post_norm_ffn[l])
        if need_ctx:
            xc = xc + cg1 * rmsnorm(oc, post_norm_mix[l])
            hc = modulate(rmsnorm(xc, pre_norm_ffn[l]), csh2, csc2)
            fc = channel_mixer(hc, l, ffn_w_gate, ffn_w_up, ffn_w_down, moe_w_router, moe_b_router,
                               moe_w_gate, moe_w_up, moe_w_down)
            xc = xc + cg2 * rmsnorm(fc, post_norm_ffn[l])
    return xl
```

```python
import functools
import math

import jax
import jax.numpy as jnp
from jax import lax
from jax.experimental import pallas as pl
from jax.experimental.pallas import tpu as pltpu

F32 = jnp.float32
BF16 = jnp.bfloat16

GRID_W = 64
ROPE_THETA = 10000.0
NORM_EPS = 1e-6
TOP_K = 2
LANES = 128
VMEM_LIMIT = 56 * 1024 * 1024
LOG2E = 1.4426950408889634


def _params(n_axes):
    return pltpu.CompilerParams(dimension_semantics=("arbitrary",) * n_axes,
                                vmem_limit_bytes=VMEM_LIMIT)


def _dot(a, b):
    return jnp.dot(a, b, preferred_element_type=F32)


def _round_up(n, m):
    return (n + m - 1) // m * m


def _tile(n, pref):
    if n <= pref:
        return n
    t = pref - pref % LANES
    while n % t:
        t -= LANES
    return t


def _ada_kernel(c_ref, w_ref, b_ref, o_ref):
    c = c_ref[...]
    s = (c * jax.nn.sigmoid(c)).astype(BF16)
    o_ref[0] = _dot(s, w_ref[0].astype(BF16)) + b_ref[0]


def _ada(cc, w_ada, b_ada):
    L, D, N = w_ada.shape
    R = cc.shape[0]
    tn = _tile(N, 512)
    return pl.pallas_call(
        _ada_kernel,
        grid=(L, N // tn),
        in_specs=[pl.BlockSpec((R, D), lambda l, j: (0, 0)),
                  pl.BlockSpec((1, D, tn), lambda l, j: (l, 0, j)),
                  pl.BlockSpec((1, 1, tn), lambda l, j: (l, 0, j))],
        out_specs=pl.BlockSpec((1, R, tn), lambda l, j: (l, 0, j)),
        out_shape=jax.ShapeDtypeStruct((L, R, N), F32),
        compiler_params=_params(2),
        name="ada",
    )(cc, w_ada, b_ada.reshape(L, 1, N))


def _top2_rows(logits, n_experts):
    lane = lax.broadcasted_iota(jnp.int32, logits.shape, 1)
    lane_f = lane.astype(F32)
    neg = jnp.float32(-jnp.inf)
    lg = jnp.where(lane < n_experts, logits, neg)
    v1 = jnp.max(lg, axis=-1, keepdims=True)
    i1 = jnp.min(jnp.where(lg == v1, lane_f, float(LANES)), axis=-1, keepdims=True)
    lg2 = jnp.where(lane_f == i1, neg, lg)
    v2 = jnp.max(lg2, axis=-1, keepdims=True)
    i2 = jnp.min(jnp.where(lg2 == v2, lane_f, float(LANES)), axis=-1, keepdims=True)
    e = jnp.exp(v2 - v1)
    w1 = 1.0 / (1.0 + e)
    w2 = e / (1.0 + e)
    return jnp.where(lane == 0, i1, jnp.where(lane == 1, i2, jnp.where(lane == 2, w1, jnp.where(lane == 3, w2, 0.0))))


def _norm_kernel(*refs, has_update, has_pre, pack_h, n_experts):
    it = iter(refs)
    x_ref = next(it)
    if has_update:
        u_ref, pw_ref, g_ref = next(it), next(it), next(it)
    if has_pre:
        w_ref, sh_ref, sc_ref = next(it), next(it), next(it)
    if n_experts:
        wr_ref, br_ref = next(it), next(it)
    if has_update:
        xo_ref = next(it)
    if has_pre:
        h_ref = next(it)
    if n_experts:
        r_ref = next(it)

    x = x_ref[...]
    if has_update:
        u = u_ref[...].astype(F32)
        un = u * lax.rsqrt(jnp.mean(u * u, axis=-1, keepdims=True) + NORM_EPS)
        x = x + g_ref[...] * (un * pw_ref[...])
        xo_ref[...] = x
    if has_pre:
        y = x * lax.rsqrt(jnp.mean(x * x, axis=-1, keepdims=True) + NORM_EPS)
        h = (y * w_ref[...]) * (1.0 + sc_ref[...]) + sh_ref[...]
        if pack_h:
            half = h.shape[1] // 2
            lo = pltpu.bitcast(h[:, :half].astype(BF16).astype(F32), jnp.uint32)
            hi = pltpu.bitcast(h[:, half:].astype(BF16).astype(F32), jnp.uint32)
            h_ref[...] = (lo >> 16) | (hi & jnp.uint32(0xFFFF0000))
        else:
            h_ref[...] = h.astype(BF16)
        if n_experts:
            wr = wr_ref[...]
            h_hi = h.astype(BF16)
            h_lo = (h - h_hi.astype(F32)).astype(BF16)
            w_hi = wr.astype(BF16)
            w_lo = (wr - w_hi.astype(F32)).astype(BF16)
            logits = _dot(h_hi, w_hi) + (_dot(h_hi, w_lo) + _dot(h_lo, w_hi)) + br_ref[...]
            r_ref[...] = _top2_rows(logits, n_experts)


def _norm_call(x, update=None, pre=None, router=None, pack_h=False):
    T, D = x.shape
    tr = min(256, T)
    row = pl.BlockSpec((tr, D), lambda i: (i, 0))
    vec = pl.BlockSpec((1, D), lambda i: (0, 0))
    args, in_specs, out_shape, out_specs = [x], [row], [], []
    if update is not None:
        upd, pw, g = update
        args += [upd, pw.reshape(1, D), g.reshape(1, D)]
        in_specs += [row, vec, vec]
        out_shape.append(jax.ShapeDtypeStruct((T, D), F32))
        out_specs.append(row)
    if pre is not None:
        w, sh, sc = pre
        args += [w.reshape(1, D), sh.reshape(1, D), sc.reshape(1, D)]
        in_specs += [vec, vec, vec]
        if pack_h:
            out_shape.append(jax.ShapeDtypeStruct((T, D // 2), jnp.uint32))
            out_specs.append(pl.BlockSpec((tr, D // 2), lambda i: (i, 0)))
        else:
            out_shape.append(jax.ShapeDtypeStruct((T, D), BF16))
            out_specs.append(row)
    n_experts = 0
    if router is not None:
        wr, br = router
        n_experts = wr.shape[1]
        wr_p = jnp.zeros((D, LANES), F32).at[:, :n_experts].set(wr)
        br_p = jnp.zeros((1, LANES), F32).at[0, :n_experts].set(br)
        args += [wr_p, br_p]
        in_specs += [pl.BlockSpec((D, LANES), lambda i: (0, 0)), pl.BlockSpec((1, LANES), lambda i: (0, 0))]
        out_shape.append(jax.ShapeDtypeStruct((T, LANES), F32))
        out_specs.append(pl.BlockSpec((tr, LANES), lambda i: (i, 0)))
    kern = functools.partial(_norm_kernel, has_update=update is not None, has_pre=pre is not None,
                             pack_h=pack_h, n_experts=n_experts)
    return pl.pallas_call(kern, grid=(T // tr,), in_specs=in_specs, out_specs=out_specs,
                          out_shape=out_shape, compiler_params=_params(1), name="norm")(*args)


def _mm_kernel(a_ref, b_ref, o_ref):
    o_ref[...] = _dot(a_ref[...], b_ref[...]).astype(o_ref.dtype)


def _mm_acc_kernel(a_ref, b_ref, o_ref, acc_ref, *, nk):
    k = pl.program_id(2)
    p = _dot(a_ref[...], b_ref[...])

    @pl.when(k == 0)
    def _():
        acc_ref[...] = p

    @pl.when(k > 0)
    def _():
        acc_ref[...] += p

    @pl.when(k == nk - 1)
    def _():
        o_ref[...] = acc_ref[...].astype(o_ref.dtype)


def _k_tile(K, limit=4096):
    if K <= limit:
        return K
    tk = limit - limit % 256
    while K % tk:
        tk -= 256
    return tk


def _matmul(a, b, out_dtype, tm=1024, tn=1024):
    M, K = a.shape
    N = b.shape[1]
    tm, tn, tk = _tile(M, tm), _tile(N, tn), _k_tile(K)
    out_shape = jax.ShapeDtypeStruct((M, N), out_dtype)
    if tk == K:
        return pl.pallas_call(
            _mm_kernel, grid=(M // tm, N // tn),
            in_specs=[pl.BlockSpec((tm, K), lambda i, j: (i, 0)), pl.BlockSpec((K, tn), lambda i, j: (0, j))],
            out_specs=pl.BlockSpec((tm, tn), lambda i, j: (i, j)),
            out_shape=out_shape, compiler_params=_params(2), name="matmul")(a, b)
    nk = K // tk
    return pl.pallas_call(
        functools.partial(_mm_acc_kernel, nk=nk), grid=(M // tm, N // tn, nk),
        in_specs=[pl.BlockSpec((tm, tk), lambda i, j, k: (i, k)), pl.BlockSpec((tk, tn), lambda i, j, k: (k, j))],
        out_specs=pl.BlockSpec((tm, tn), lambda i, j, k: (i, j)),
        out_shape=out_shape, scratch_shapes=[pltpu.VMEM((tm, tn), F32)],
        compiler_params=_params(3), name="matmul_acc")(a, b)


def _swiglu_kernel(a_ref, wg_ref, wu_ref, o_ref):
    a = a_ref[...]
    g = _dot(a, wg_ref[...])
    u = _dot(a, wu_ref[...])
    o_ref[...] = (g * jax.nn.sigmoid(g) * u).astype(o_ref.dtype)


def _swiglu_up(a, wg, wu, tm=1024, tn=512):
    M, K = a.shape
    N = wg.shape[1]
    tm, tn = _tile(M, tm), _tile(N, tn)
    wspec = pl.BlockSpec((K, tn), lambda i, j: (0, j))
    return pl.pallas_call(
        _swiglu_kernel, grid=(M // tm, N // tn),
        in_specs=[pl.BlockSpec((tm, K), lambda i, j: (i, 0)), wspec, wspec],
        out_specs=pl.BlockSpec((tm, tn), lambda i, j: (i, j)),
        out_shape=jax.ShapeDtypeStruct((M, N), BF16), compiler_params=_params(2), name="swiglu_up")(a, wg, wu)


def _merge_kernel(a1_ref, a2_ref, w1_ref, w2_ref, g1_ref, g2_ref, o_ref):
    y1 = _dot(a1_ref[...], w1_ref[...])
    y2 = _dot(a2_ref[...], w2_ref[...])
    g1 = jax.nn.sigmoid(g1_ref[...].astype(F32))
    g2 = jax.nn.sigmoid(g2_ref[...].astype(F32))
    o_ref[...] = (g1 * y1 + g2 * y2).astype(o_ref.dtype)


def _gated_merge(attn, gm, proj, ga_col, gb_col, w_pa, w_pb, tm=1024, tn=512):
    M, K1 = attn.shape
    K2 = gm.shape[1]
    N = w_pa.shape[1]
    tm, tn = _tile(M, tm), _tile(N, tn)
    ja, jb = ga_col // tn, gb_col // tn
    return pl.pallas_call(
        _merge_kernel, grid=(M // tm, N // tn),
        in_specs=[pl.BlockSpec((tm, K1), lambda i, j: (i, 0)),
                  pl.BlockSpec((tm, K2), lambda i, j: (i, 0)),
                  pl.BlockSpec((K1, tn), lambda i, j: (0, j)),
                  pl.BlockSpec((K2, tn), lambda i, j: (0, j)),
                  pl.BlockSpec((tm, tn), lambda i, j: (i, ja + j)),
                  pl.BlockSpec((tm, tn), lambda i, j: (i, jb + j))],
        out_specs=pl.BlockSpec((tm, tn), lambda i, j: (i, j)),
        out_shape=jax.ShapeDtypeStruct((M, N), BF16), compiler_params=_params(2), name="gated_merge",
    )(attn, gm, w_pa, w_pb, proj, proj)


def _prep_kernel(*refs, heads, hd, rope, qscale):
    if rope:
        q_ref, k_ref, v_ref, cos_ref, sin_ref, qT_ref, ko_ref, vT_ref = refs
        cos, sin = cos_ref[...], sin_ref[...]
        lane = lax.broadcasted_iota(jnp.int32, cos.shape, 1)
        first = (lane % 32) < 16

        def rot(t):
            partner = jnp.where(first, pltpu.roll(t, hd - 16, 1), pltpu.roll(t, 16, 1))
            return t * cos + partner * sin
    else:
        q_ref, k_ref, v_ref, qT_ref, ko_ref, vT_ref = refs

        def rot(t):
            return t
    for h in range(heads):
        sl = slice(h * hd, (h + 1) * hd)
        q = rot(q_ref[:, sl].astype(F32)) * qscale
        qT_ref[sl, :] = q.T.astype(BF16)
        ko_ref[:, sl] = rot(k_ref[:, sl].astype(F32)).astype(BF16)
        vT_ref[sl, :] = v_ref[:, sl].astype(F32).T.astype(BF16)


def _prep(proj, width, hd, qscale, tables=None):
    T = proj.shape[0]
    tr = min(256, T)
    heads = width // hd
    col = lambda c: pl.BlockSpec((tr, width), lambda i: (i, c))
    args, in_specs = [proj, proj, proj], [col(0), col(1), col(2)]
    if tables is not None:
        args += list(tables)
        in_specs += [pl.BlockSpec((tr, hd), lambda i: (i, 0))] * 2
    tspec = pl.BlockSpec((width, tr), lambda i: (0, i))
    return pl.pallas_call(
        functools.partial(_prep_kernel, heads=heads, hd=hd, rope=tables is not None, qscale=qscale),
        grid=(T // tr,), in_specs=in_specs,
        out_specs=[tspec, pl.BlockSpec((tr, width), lambda i: (i, 0)), tspec],
        out_shape=[jax.ShapeDtypeStruct((width, T), BF16), jax.ShapeDtypeStruct((T, width), BF16),
                   jax.ShapeDtypeStruct((width, T), BF16)],
        compiler_params=_params(1), name="prep")(*args)


def _rope_tables(n_tokens, hd):
    freqs = hd // 8
    t = jnp.arange(n_tokens)
    inv = 1.0 / (ROPE_THETA ** (jnp.arange(freqs, dtype=F32) / freqs))
    ar = (t // GRID_W).astype(F32)[:, None] * inv
    ac = (t % GRID_W).astype(F32)[:, None] * inv
    cos = jnp.concatenate([jnp.cos(ar), jnp.cos(ar), jnp.cos(ac), jnp.cos(ac)] * 2, axis=1)
    sin = jnp.concatenate([-jnp.sin(ar), jnp.sin(ar), -jnp.sin(ac), jnp.sin(ac)] * 2, axis=1)
    return cos, sin


def _attn_kernel(*refs, has_lat, nkv, tq, dk, out_scale):
    if has_lat:
        lam_ref, qT_ref, k_ref, vT_ref, kc_ref, vTc_ref, sub_ref, o_ref, qbd_ref, m_ref, l_ref, acc_ref = refs
    else:
        lam_ref, qT_ref, kc_ref, vTc_ref, sub_ref, o_ref, qbd_ref, m_ref, l_ref, acc_ref = refs
    j = pl.program_id(2)

    @pl.when(j == 0)
    def _init():
        z = jnp.zeros((dk, tq), BF16)
        qbd_ref[0:dk, 0:tq] = qT_ref[0:dk, :]
        qbd_ref[0:dk, tq:2 * tq] = z
        qbd_ref[dk:2 * dk, 0:tq] = z
        qbd_ref[dk:2 * dk, tq:2 * tq] = qT_ref[dk:2 * dk, :]
        m_ref[...] = jnp.full(m_ref.shape, -jnp.inf, F32)
        l_ref[...] = jnp.zeros(l_ref.shape, F32)
        acc_ref[...] = jnp.zeros(acc_ref.shape, F32)

    def step(k, vT):
        s = _dot(k, qbd_ref[...])
        m_prev = m_ref[...]
        m_new = jnp.maximum(m_prev, jnp.max(s, axis=0, keepdims=True))
        alpha = jnp.exp2(m_prev - m_new)
        p = jnp.exp2(s - m_new)
        l_ref[...] = alpha * l_ref[...] + jnp.sum(p, axis=0, keepdims=True)
        acc_ref[...] = alpha * acc_ref[...] + _dot(vT, p.astype(BF16))
        m_ref[...] = m_new

    if has_lat:
        @pl.when(j < nkv)
        def _lat():
            step(k_ref[...], vT_ref[...])

    @pl.when(j == nkv)
    def _last():
        step(kc_ref[...], vTc_ref[...])
        o_both = acc_ref[...] / l_ref[...]
        o = o_both[:, 0:tq] - lam_ref[0] * o_both[:, tq:2 * tq]
        y = o * lax.rsqrt(jnp.mean(o * o, axis=0, keepdims=True) + NORM_EPS)
        y = y * (sub_ref[...] * out_scale)
        o_ref[...] = y.T.astype(o_ref.dtype)


def _attention(lam, qT, kc, vTc, subln, out_scale, k=None, vT=None, tq=512, tk=512):
    W, Sq = qT.shape
    hd = subln.shape[0]
    H = W // hd
    C = kc.shape[0]
    tq = min(tq, Sq)
    has_lat = k is not None
    nkv = 0
    args = [lam, qT]
    in_specs = [pl.BlockSpec(memory_space=pltpu.SMEM), pl.BlockSpec((hd, tq), lambda h, i, j: (h, i))]
    if has_lat:
        S = k.shape[0]
        tk = min(tk, S)
        nkv = S // tk
        args += [k, vT]
        in_specs += [pl.BlockSpec((tk, hd), lambda h, i, j: (jnp.minimum(j, nkv - 1), h)),
                     pl.BlockSpec((hd, tk), lambda h, i, j: (h, jnp.minimum(j, nkv - 1)))]
    args += [kc, vTc, subln.reshape(hd, 1)]
    in_specs += [pl.BlockSpec((C, hd), lambda h, i, j: (0, h)),
                 pl.BlockSpec((hd, C), lambda h, i, j: (h, 0)),
                 pl.BlockSpec((hd, 1), lambda h, i, j: (0, 0))]
    return pl.pallas_call(
        functools.partial(_attn_kernel, has_lat=has_lat, nkv=nkv, tq=tq, dk=hd // 2, out_scale=out_scale),
        grid=(H, Sq // tq, nkv + 1), in_specs=in_specs,
        out_specs=pl.BlockSpec((tq, hd), lambda h, i, j: (i, h)),
        out_shape=jax.ShapeDtypeStruct((Sq, W), BF16),
        scratch_shapes=[pltpu.VMEM((hd, 2 * tq), BF16), pltpu.VMEM((1, 2 * tq), F32),
                        pltpu.VMEM((1, 2 * tq), F32), pltpu.VMEM((hd, 2 * tq), F32)],
        compiler_params=_params(3), name="diff_attention")(*args)


def _gmlp_kernel(u_ref, v_ref, gw_ref, gb_ref, ws_ref, bs_ref, o_ref, *, groups, chunk, gdim):
    v = v_ref[...].astype(F32)
    vc = v - jnp.mean(v, axis=-1, keepdims=True)
    vn = vc * lax.rsqrt(jnp.mean(vc * vc, axis=-1, keepdims=True) + NORM_EPS)
    vn = (vn * gw_ref[...] + gb_ref[...]).astype(BF16)
    n_chunks = v.shape[0] // chunk
    for g in range(groups):
        cols = slice(g * gdim, (g + 1) * gdim)
        rhs = jnp.concatenate([vn[c * chunk:(c + 1) * chunk, cols] for c in range(n_chunks)], axis=1)
        mixed = _dot(ws_ref[g], rhs)
        for c in range(n_chunks):
            rows = slice(c * chunk, (c + 1) * chunk)
            m = mixed[:, c * gdim:(c + 1) * gdim] + bs_ref[g]
            o_ref[rows, cols] = (u_ref[rows, cols].astype(F32) * m).astype(o_ref.dtype)


def _spatial_gating(proj, u_col, v_col, gn_w, gn_b, w_s, b_s):
    T = proj.shape[0]
    G, chunk, _ = w_s.shape
    GW = gn_w.shape[0]
    gdim = GW // G
    tr = min(512, T)
    bias = jnp.broadcast_to(b_s[:, :, None], (G, chunk, gdim)).astype(F32)
    vec = pl.BlockSpec((1, GW), lambda i: (0, 0))
    return pl.pallas_call(
        functools.partial(_gmlp_kernel, groups=G, chunk=chunk, gdim=gdim),
        grid=(T // tr,),
        in_specs=[pl.BlockSpec((tr, GW), lambda i: (i, u_col // GW)),
                  pl.BlockSpec((tr, GW), lambda i: (i, v_col // GW)),
                  vec, vec,
                  pl.BlockSpec((G, chunk, chunk), lambda i: (0, 0, 0)),
                  pl.BlockSpec((G, chunk, gdim), lambda i: (0, 0, 0))],
        out_specs=pl.BlockSpec((tr, GW), lambda i: (i, 0)),
        out_shape=jax.ShapeDtypeStruct((T, GW), BF16),
        compiler_params=_params(1), name="spatial_gating",
    )(proj, proj, gn_w.reshape(1, GW), gn_b.reshape(1, GW), w_s.astype(BF16), bias)


def _gather_rows_kernel(src_ref, nrows_ref, h_hbm, o_ref, sem, *, tg):
    base = pl.program_id(0) * tg

    @pl.when(base < nrows_ref[0])
    def _():
        def start(r, carry):
            pltpu.make_async_copy(h_hbm.at[pl.ds(src_ref[base + r], 1)], o_ref.at[pl.ds(r, 1)], sem).start()
            return carry

        def wait(r, carry):
            pltpu.make_async_copy(h_hbm.at[pl.ds(0, 1)], o_ref.at[pl.ds(r, 1)], sem).wait()
            return carry

        lax.fori_loop(0, tg, start, 0)
        lax.fori_loop(0, tg, wait, 0)

    @pl.when(base >= nrows_ref[0])
    def _():
        o_ref[...] = jnp.zeros(o_ref.shape, o_ref.dtype)


def _gather_rows(h, src, n_rows, R, tg=256):
    W = h.shape[1]
    return pl.pallas_call(
        functools.partial(_gather_rows_kernel, tg=tg),
        grid_spec=pltpu.PrefetchScalarGridSpec(
            num_scalar_prefetch=2, grid=(R // tg,),
            in_specs=[pl.BlockSpec(memory_space=pl.ANY)],
            out_specs=pl.BlockSpec((tg, W), lambda i, src, n: (i, 0)),
            scratch_shapes=[pltpu.SemaphoreType.DMA(())]),
        out_shape=jax.ShapeDtypeStruct((R, W), h.dtype),
        compiler_params=_params(1), name="moe_gather")(src, n_rows, h)


def _unpack_bf16_pairs(w):
    lo = pltpu.bitcast(w << 16, F32).astype(BF16)
    hi = pltpu.bitcast(w & jnp.uint32(0xFFFF0000), F32).astype(BF16)
    return jnp.concatenate([lo, hi], axis=1)


def _moe_up_kernel(te_ref, nt_ref, x_ref, wg_ref, wu_ref, o_ref, xb_ref):
    t, j = pl.program_id(0), pl.program_id(1)

    @pl.when(t < nt_ref[0])
    def _():
        @pl.when(j == 0)
        def _():
            xb_ref[...] = _unpack_bf16_pairs(x_ref[...])

        a = xb_ref[...]
        g = _dot(a, wg_ref[0])
        u = _dot(a, wu_ref[0])
        o_ref[...] = (g * jax.nn.sigmoid(g) * u).astype(o_ref.dtype)

    @pl.when(t >= nt_ref[0])
    def _():
        o_ref[...] = jnp.zeros(o_ref.shape, o_ref.dtype)


def _moe_down_kernel(te_ref, nt_ref, a_ref, w_ref, o_ref):
    @pl.when(pl.program_id(0) < nt_ref[0])
    def _():
        o_ref[...] = _dot(a_ref[...], w_ref[0]).astype(o_ref.dtype)

    @pl.when(pl.program_id(0) >= nt_ref[0])
    def _():
        o_ref[...] = jnp.zeros(o_ref.shape, o_ref.dtype)


def _moe_experts(xs, tile_expert, n_tiles, wg, wu, wd, tm, tf=512, tn=1024):
    R = xs.shape[0]
    E, D, F = wg.shape
    NT = R // tm
    tf, tn = _tile(F, tf), _tile(D, tn)
    tile = lambda t, nt: jnp.minimum(t, nt[0] - 1)

    def wmap(last_j):
        return lambda t, j, te, nt: (te[t], 0, jnp.where(t < nt[0], j, last_j))

    hidden = pl.pallas_call(
        _moe_up_kernel,
        grid_spec=pltpu.PrefetchScalarGridSpec(
            num_scalar_prefetch=2, grid=(NT, F // tf),
            in_specs=[pl.BlockSpec((tm, D // 2), lambda t, j, te, nt: (tile(t, nt), 0)),
                      pl.BlockSpec((1, D, tf), wmap(F // tf - 1)),
                      pl.BlockSpec((1, D, tf), wmap(F // tf - 1))],
            out_specs=pl.BlockSpec((tm, tf), lambda t, j, te, nt: (t, j)),
            scratch_shapes=[pltpu.VMEM((tm, D), BF16)]),
        out_shape=jax.ShapeDtypeStruct((R, F), BF16),
        compiler_params=_params(2), name="moe_up")(tile_expert, n_tiles, xs, wg, wu)
    return pl.pallas_call(
        _moe_down_kernel,
        grid_spec=pltpu.PrefetchScalarGridSpec(
            num_scalar_prefetch=2, grid=(NT, D // tn),
            in_specs=[pl.BlockSpec((tm, F), lambda t, j, te, nt: (tile(t, nt), 0)),
                      pl.BlockSpec((1, F, tn), wmap(D // tn - 1))],
            out_specs=pl.BlockSpec((tm, tn), lambda t, j, te, nt: (t, j))),
        out_shape=jax.ShapeDtypeStruct((R, D), F32),
        compiler_params=_params(2), name="moe_down")(tile_expert, n_tiles, hidden, wd)


def _moe_combine_kernel(pos_ref, ys_hbm, x_ref, wt_ref, pw_ref, g_ref, o_ref, buf_ref, sem, *, tr):
    base = pl.program_id(0) * tr * TOP_K

    def start(r, carry):
        for kk in range(TOP_K):
            pltpu.make_async_copy(ys_hbm.at[pl.ds(pos_ref[base + TOP_K * r + kk], 1)],
                                  buf_ref.at[kk, pl.ds(r, 1)], sem).start()
        return carry

    def wait(r, carry):
        for kk in range(TOP_K):
            pltpu.make_async_copy(ys_hbm.at[pl.ds(0, 1)], buf_ref.at[kk, pl.ds(r, 1)], sem).wait()
        return carry

    lax.fori_loop(0, tr, start, 0)
    lax.fori_loop(0, tr, wait, 0)
    wt = wt_ref[...]
    f = wt[:, 2:3] * buf_ref[0] + wt[:, 3:4] * buf_ref[1]
    fn = f * lax.rsqrt(jnp.mean(f * f, axis=-1, keepdims=True) + NORM_EPS)
    o_ref[...] = x_ref[...] + g_ref[...] * (fn * pw_ref[...])


def _moe_combine(pos, ys, x, route, post_w, gate, tr=128):
    T, D = x.shape
    tr = min(tr, T)
    vec = pl.BlockSpec((1, D), lambda i, p: (0, 0))
    return pl.pallas_call(
        functools.partial(_moe_combine_kernel, tr=tr),
        grid_spec=pltpu.PrefetchScalarGridSpec(
            num_scalar_prefetch=1, grid=(T // tr,),
            in_specs=[pl.BlockSpec(memory_space=pl.ANY),
                      pl.BlockSpec((tr, D), lambda i, p: (i, 0)),
                      pl.BlockSpec((tr, LANES), lambda i, p: (i, 0)),
                      vec, vec],
            out_specs=pl.BlockSpec((tr, D), lambda i, p: (i, 0)),
            scratch_shapes=[pltpu.VMEM((TOP_K, tr, D), F32), pltpu.SemaphoreType.DMA(())]),
        out_shape=jax.ShapeDtypeStruct((T, D), F32),
        compiler_params=_params(1), name="moe_combine",
    )(pos, ys, x, route, post_w.reshape(1, D), gate.reshape(1, D))


def _moe_layer(x, h_packed, route, wg, wu, wd, post_w, gate, tm=512):
    T = x.shape[0]
    E = wg.shape[0]
    tm = min(tm, T)
    A = T * TOP_K
    R = _round_up(A, tm) + E * tm
    expert = route[:, :TOP_K].astype(jnp.int32).reshape(A)
    onehot = (expert[:, None] == jnp.arange(E, dtype=jnp.int32)[None, :]).astype(jnp.int32)
    csum = jnp.cumsum(onehot, axis=0)
    counts = csum[-1]
    rank = jnp.sum((csum - 1) * onehot, axis=1)
    padded = (counts + tm - 1) // tm * tm
    ends = jnp.cumsum(padded)
    offsets = ends - padded
    pos = (jnp.sum(onehot * offsets[None, :], axis=1) + rank).astype(jnp.int32)
    src = jnp.zeros((R,), jnp.int32).at[pos].set(jnp.arange(A, dtype=jnp.int32) // TOP_K)
    n_rows = ends[-1:].astype(jnp.int32)
    n_tiles = n_rows // tm
    tile_start = jnp.arange(R // tm, dtype=jnp.int32) * tm
    tile_expert = jnp.sum((tile_start[:, None] >= ends[None, :]).astype(jnp.int32), axis=1)
    tile_expert = jnp.minimum(tile_expert, jnp.max(jnp.where(counts > 0, jnp.arange(E), 0))).astype(jnp.int32)
    xs = _gather_rows(h_packed, src, n_rows, R)
    ys = _moe_experts(xs, tile_expert, n_tiles, wg, wu, wd, tm)
    return _moe_combine(pos, ys, x, route, post_w, gate)


def _token_mixer(h, p, qscale, tables):
    proj = _matmul(h, p["w_in"], BF16)
    qT, k, vT = _prep(proj, p["qk_w"], p["hd"], qscale, tables)
    return proj, qT, k, vT


def kernel(x, c, ctx, c_ctx, w_ada, b_ada, pre_norm_mix, post_norm_mix, pre_norm_ffn, post_norm_ffn, w_in, lambda_q1, lambda_k1, lambda_q2, lambda_k2, da_subln, gm_norm_w, gm_norm_b, gm_w_s, gm_b_s, w_branch_attn, w_branch_gmlp, w_out, ffn_w_gate, ffn_w_up, ffn_w_down, moe_w_router, moe_b_router, moe_w_gate, moe_w_up, moe_w_down):
    B, S, D = x.shape
    assert B == 1, "the kernels process one sequence"
    L = w_ada.shape[0]
    hd = da_subln.shape[1]
    dk = lambda_q1.shape[1]
    assert hd == 2 * dk == LANES
    qk_w = w_branch_attn.shape[1]
    gm_w = gm_norm_w.shape[1]
    u_col, v_col = 3 * qk_w, 3 * qk_w + gm_w
    ga_col, gb_col = 3 * qk_w + 2 * gm_w, 3 * qk_w + 2 * gm_w + D
    assert w_in.shape[2] == gb_col + D
    qscale = dk ** -0.5 * LOG2E

    xl, xc = x[0], ctx[0]
    cc = jnp.zeros((8, D), F32).at[0].set(c[0]).at[1].set(c_ctx)
    mod = _ada(cc, w_ada, b_ada)
    mod_l = mod[:, 0].reshape(L, 6, D)
    mod_c = mod[:, 1].reshape(L, 6, D)
    tables = _rope_tables(S, hd)

    hl = _norm_call(xl, pre=(pre_norm_mix[0], mod_l[0, 0], mod_l[0, 1]))[0]
    hc = _norm_call(xc, pre=(pre_norm_mix[0], mod_c[0, 0], mod_c[0, 1]))[0]

    for l in range(L):
        need_ctx = l < L - 1
        last = l == L - 1
        lambda_init = 0.8 - 0.6 * math.exp(-0.3 * l)
        lam = (jnp.exp(jnp.sum(lambda_q1[l] * lambda_k1[l])) - jnp.exp(jnp.sum(lambda_q2[l] * lambda_k2[l]))
               + lambda_init).reshape(1).astype(F32)
        p = {"w_in": w_in[l].astype(BF16), "qk_w": qk_w, "hd": hd}
        w_pa, w_pb, w_o = w_branch_attn[l].astype(BF16), w_branch_gmlp[l].astype(BF16), w_out[l].astype(BF16)
        out_scale = 1.0 - lambda_init
        is_moe = l % 2 == 1
        i = l // 2

        proj_l, qT_l, k_l, vT_l = _token_mixer(hl, p, qscale, tables)
        proj_c, qT_c, k_c, vT_c = _token_mixer(hc, p, qscale, None)
        attn_l = _attention(lam, qT_l, k_c, vT_c, da_subln[l], out_scale, k=k_l, vT=vT_l)
        gm_l = _spatial_gating(proj_l, u_col, v_col, gm_norm_w[l], gm_norm_b[l], gm_w_s[l], gm_b_s[l])
        ol = _matmul(_gated_merge(attn_l, gm_l, proj_l, ga_col, gb_col, w_pa, w_pb), w_o, F32)
        if need_ctx:
            attn_c = _attention(lam, qT_c, k_c, vT_c, da_subln[l], out_scale)
            gm_c = _spatial_gating(proj_c, u_col, v_col, gm_norm_w[l], gm_norm_b[l], gm_w_s[l], gm_b_s[l])
            oc = _matmul(_gated_merge(attn_c, gm_c, proj_c, ga_col, gb_col, w_pa, w_pb), w_o, F32)

        def mixer_update(xr, o, m):
            router = (moe_w_router[i], moe_b_router[i]) if is_moe else None
            return _norm_call(xr, update=(o, post_norm_mix[l], m[l, 2]),
                              pre=(pre_norm_ffn[l], m[l, 3], m[l, 4]), router=router, pack_h=is_moe)

        def next_pre(m):
            return None if last else (pre_norm_mix[l + 1], m[l + 1, 0], m[l + 1, 1])

        if is_moe:
            wg, wu, wd = moe_w_gate[i].astype(BF16), moe_w_up[i].astype(BF16), moe_w_down[i].astype(BF16)
            xl, hl2, route = mixer_update(xl, ol, mod_l)
            xl = _moe_layer(xl, hl2, route, wg, wu, wd, post_norm_ffn[l], mod_l[l, 5])
            if not last:
                hl = _norm_call(xl, pre=next_pre(mod_l))[0]
            if need_ctx:
                xc, hc2, route_c = mixer_update(xc, oc, mod_c)
                xc = _moe_layer(xc, hc2, route_c, wg, wu, wd, post_norm_ffn[l], mod_c[l, 5])
                hc = _norm_call(xc, pre=next_pre(mod_c))[0]
        else:
            FF = ffn_w_gate.shape[2]
            FFp = _round_up(FF, 1024)
            wg = jnp.zeros((D, FFp), BF16).at[:, :FF].set(ffn_w_gate[i].astype(BF16))
            wu = jnp.zeros((D, FFp), BF16).at[:, :FF].set(ffn_w_up[i].astype(BF16))
            wd = jnp.zeros((FFp, D), BF16).at[:FF].set(ffn_w_down[i].astype(BF16))

            def dense(xr, o, m):
                xr, h2 = mixer_update(xr, o, m)
                f = _matmul(_swiglu_up(h2, wg, wu), wd, F32)
                res = _norm_call(xr, update=(f, post_norm_ffn[l], m[l, 5]), pre=next_pre(m))
                return res[0], (res[1] if not last else None)

            xl, hl = dense(xl, ol, mod_l)
            if need_ctx:
                xc, hc = dense(xc, oc, mod_c)
            elif not last:
                hc = _norm_call(xc, pre=next_pre(mod_c))[0]
    return xl[None]
```

```python
import functools
import math

import jax
import jax.numpy as jnp
from jax import lax
from jax.experimental import pallas as pl
from jax.experimental.pallas import tpu as pltpu

F32 = jnp.float32
BF16 = jnp.bfloat16

GRID_W = 64
ROPE_THETA = 10000.0
NORM_EPS = 1e-6
TOP_K = 2
LANES = 128
MXU_COLS = 256
VMEM_LIMIT = 56 * 1024 * 1024
LOG2E = 1.4426950408889634


def _params(n_axes):
    return pltpu.CompilerParams(dimension_semantics=("arbitrary",) * n_axes,
                                vmem_limit_bytes=VMEM_LIMIT)


def _dot(a, b):
    return jnp.dot(a, b, preferred_element_type=F32)


def _round_up(n, m):
    return (n + m - 1) // m * m


def _tile(n, pref):
    if n <= pref:
        return n
    t = pref - pref % LANES
    while n % t:
        t -= LANES
    return t


def _ada_kernel(c_ref, w_ref, b_ref, o_ref):
    c = c_ref[...]
    s = (c * jax.nn.sigmoid(c)).astype(BF16)
    o_ref[0] = _dot(s, w_ref[0].astype(BF16)) + b_ref[0]


def _ada(cc, w_ada, b_ada):
    L, D, N = w_ada.shape
    R = cc.shape[0]
    tn = _tile(N, 512)
    return pl.pallas_call(
        _ada_kernel,
        grid=(L, N // tn),
        in_specs=[pl.BlockSpec((R, D), lambda l, j: (0, 0)),
                  pl.BlockSpec((1, D, tn), lambda l, j: (l, 0, j)),
                  pl.BlockSpec((1, 1, tn), lambda l, j: (l, 0, j))],
        out_specs=pl.BlockSpec((1, R, tn), lambda l, j: (l, 0, j)),
        out_shape=jax.ShapeDtypeStruct((L, R, N), F32),
        compiler_params=_params(2),
        name="ada",
    )(cc, w_ada, b_ada.reshape(L, 1, N))


def _top2_rows(logits, n_experts):
    lane = lax.broadcasted_iota(jnp.int32, logits.shape, 1)
    lane_f = lane.astype(F32)
    neg = jnp.float32(-jnp.inf)
    lg = jnp.where(lane < n_experts, logits, neg)
    v1 = jnp.max(lg, axis=-1, keepdims=True)
    i1 = jnp.min(jnp.where(lg == v1, lane_f, float(LANES)), axis=-1, keepdims=True)
    lg2 = jnp.where(lane_f == i1, neg, lg)
    v2 = jnp.max(lg2, axis=-1, keepdims=True)
    i2 = jnp.min(jnp.where(lg2 == v2, lane_f, float(LANES)), axis=-1, keepdims=True)
    e = jnp.exp(v2 - v1)
    w1 = 1.0 / (1.0 + e)
    w2 = e / (1.0 + e)
    return jnp.where(lane == 0, i1, jnp.where(lane == 1, i2, jnp.where(lane == 2, w1, jnp.where(lane == 3, w2, 0.0))))


def _norm_kernel(*refs, has_update, has_pre, pack_h, n_experts):
    it = iter(refs)
    x_ref = next(it)
    if has_update:
        u_ref, pw_ref, g_ref = next(it), next(it), next(it)
    if has_pre:
        w_ref, sh_ref, sc_ref = next(it), next(it), next(it)
    if n_experts:
        wr_ref, br_ref = next(it), next(it)
    if has_update:
        xo_ref = next(it)
    if has_pre:
        h_ref = next(it)
    if n_experts:
        r_ref = next(it)

    x = x_ref[...]
    if has_update:
        u = u_ref[...].astype(F32)
        un = u * lax.rsqrt(jnp.mean(u * u, axis=-1, keepdims=True) + NORM_EPS)
        x = x + g_ref[...] * (un * pw_ref[...])
        xo_ref[...] = x
    if has_pre:
        y = x * lax.rsqrt(jnp.mean(x * x, axis=-1, keepdims=True) + NORM_EPS)
        h = (y * w_ref[...]) * (1.0 + sc_ref[...]) + sh_ref[...]
        if pack_h:
            half = h.shape[1] // 2
            lo = pltpu.bitcast(h[:, :half].astype(BF16).astype(F32), jnp.uint32)
            hi = pltpu.bitcast(h[:, half:].astype(BF16).astype(F32), jnp.uint32)
            h_ref[...] = (lo >> 16) | (hi & jnp.uint32(0xFFFF0000))
        else:
            h_ref[...] = h.astype(BF16)
        if n_experts:
            wr = wr_ref[...]
            h_hi = h.astype(BF16)
            h_lo = (h - h_hi.astype(F32)).astype(BF16)
            w_hi = wr.astype(BF16)
            w_lo = (wr - w_hi.astype(F32)).astype(BF16)
            logits = _dot(h_hi, w_hi) + (_dot(h_hi, w_lo) + _dot(h_lo, w_hi)) + br_ref[...]
            r_ref[...] = _top2_rows(logits, n_experts)


def _norm_call(x, update=None, pre=None, router=None, pack_h=False):
    T, D = x.shape
    tr = min(256, T)
    row = pl.BlockSpec((tr, D), lambda i: (i, 0))
    vec = pl.BlockSpec((1, D), lambda i: (0, 0))
    args, in_specs, out_shape, out_specs = [x], [row], [], []
    if update is not None:
        upd, pw, g = update
        args += [upd, pw.reshape(1, D), g.reshape(1, D)]
        in_specs += [row, vec, vec]
        out_shape.append(jax.ShapeDtypeStruct((T, D), F32))
        out_specs.append(row)
    if pre is not None:
        w, sh, sc = pre
        args += [w.reshape(1, D), sh.reshape(1, D), sc.reshape(1, D)]
        in_specs += [vec, vec, vec]
        if pack_h:
            out_shape.append(jax.ShapeDtypeStruct((T, D // 2), jnp.uint32))
            out_specs.append(pl.BlockSpec((tr, D // 2), lambda i: (i, 0)))
        else:
            out_shape.append(jax.ShapeDtypeStruct((T, D), BF16))
            out_specs.append(row)
    n_experts = 0
    if router is not None:
        wr, br = router
        n_experts = wr.shape[1]
        wr_p = jnp.zeros((D, LANES), F32).at[:, :n_experts].set(wr)
        br_p = jnp.zeros((1, LANES), F32).at[0, :n_experts].set(br)
        args += [wr_p, br_p]
        in_specs += [pl.BlockSpec((D, LANES), lambda i: (0, 0)), pl.BlockSpec((1, LANES), lambda i: (0, 0))]
        out_shape.append(jax.ShapeDtypeStruct((T, LANES), F32))
        out_specs.append(pl.BlockSpec((tr, LANES), lambda i: (i, 0)))
    kern = functools.partial(_norm_kernel, has_update=update is not None, has_pre=pre is not None,
                             pack_h=pack_h, n_experts=n_experts)
    return pl.pallas_call(kern, grid=(T // tr,), in_specs=in_specs, out_specs=out_specs,
                          out_shape=out_shape, compiler_params=_params(1), name="norm")(*args)


def _mm_kernel(a_ref, b_ref, o_ref):
    o_ref[...] = _dot(a_ref[...], b_ref[...]).astype(o_ref.dtype)


def _mm_acc_kernel(a_ref, b_ref, o_ref, acc_ref, *, nk):
    k = pl.program_id(2)
    p = _dot(a_ref[...], b_ref[...])

    @pl.when(k == 0)
    def _():
        acc_ref[...] = p

    @pl.when(k > 0)
    def _():
        acc_ref[...] += p

    @pl.when(k == nk - 1)
    def _():
        o_ref[...] = acc_ref[...].astype(o_ref.dtype)


def _k_tile(K, limit=4096):
    if K <= limit:
        return K
    tk = limit - limit % 256
    while K % tk:
        tk -= 256
    return tk


def _matmul(a, b, out_dtype, tm=1024, tn=1024):
    M, K = a.shape
    N = b.shape[1]
    tm, tn, tk = _tile(M, tm), _tile(N, tn), _k_tile(K)
    out_shape = jax.ShapeDtypeStruct((M, N), out_dtype)
    if tk == K:
        return pl.pallas_call(
            _mm_kernel, grid=(M // tm, N // tn),
            in_specs=[pl.BlockSpec((tm, K), lambda i, j: (i, 0)), pl.BlockSpec((K, tn), lambda i, j: (0, j))],
            out_specs=pl.BlockSpec((tm, tn), lambda i, j: (i, j)),
            out_shape=out_shape, compiler_params=_params(2), name="matmul")(a, b)
    nk = K // tk
    return pl.pallas_call(
        functools.partial(_mm_acc_kernel, nk=nk), grid=(M // tm, N // tn, nk),
        in_specs=[pl.BlockSpec((tm, tk), lambda i, j, k: (i, k)), pl.BlockSpec((tk, tn), lambda i, j, k: (k, j))],
        out_specs=pl.BlockSpec((tm, tn), lambda i, j, k: (i, j)),
        out_shape=out_shape, scratch_shapes=[pltpu.VMEM((tm, tn), F32)],
        compiler_params=_params(3), name="matmul_acc")(a, b)


def _swiglu_kernel(a_ref, wg_ref, wu_ref, o_ref):
    a = a_ref[...]
    g = _dot(a, wg_ref[...])
    u = _dot(a, wu_ref[...])
    o_ref[...] = (g * jax.nn.sigmoid(g) * u).astype(o_ref.dtype)


def _swiglu_up(a, wg, wu, tm=1024, tn=512):
    M, K = a.shape
    N = wg.shape[1]
    tm, tn = _tile(M, tm), _tile(N, tn)
    wspec = pl.BlockSpec((K, tn), lambda i, j: (0, j))
    return pl.pallas_call(
        _swiglu_kernel, grid=(M // tm, N // tn),
        in_specs=[pl.BlockSpec((tm, K), lambda i, j: (i, 0)), wspec, wspec],
        out_specs=pl.BlockSpec((tm, tn), lambda i, j: (i, j)),
        out_shape=jax.ShapeDtypeStruct((M, N), BF16), compiler_params=_params(2), name="swiglu_up")(a, wg, wu)


def _merge_kernel(a1_ref, a2_ref, w1_ref, w2_ref, g1_ref, g2_ref, o_ref):
    y1 = _dot(a1_ref[...], w1_ref[...])
    y2 = _dot(a2_ref[...], w2_ref[...])
    g1 = jax.nn.sigmoid(g1_ref[...].astype(F32))
    g2 = jax.nn.sigmoid(g2_ref[...].astype(F32))
    o_ref[...] = (g1 * y1 + g2 * y2).astype(o_ref.dtype)


def _gated_merge(attn, gm, proj, ga_col, gb_col, w_pa, w_pb, tm=1024, tn=512):
    M, K1 = attn.shape
    K2 = gm.shape[1]
    N = w_pa.shape[1]
    tm, tn = _tile(M, tm), _tile(N, tn)
    ja, jb = ga_col // tn, gb_col // tn
    return pl.pallas_call(
        _merge_kernel, grid=(M // tm, N // tn),
        in_specs=[pl.BlockSpec((tm, K1), lambda i, j: (i, 0)),
                  pl.BlockSpec((tm, K2), lambda i, j: (i, 0)),
                  pl.BlockSpec((K1, tn), lambda i, j: (0, j)),
                  pl.BlockSpec((K2, tn), lambda i, j: (0, j)),
                  pl.BlockSpec((tm, tn), lambda i, j: (i, ja + j)),
                  pl.BlockSpec((tm, tn), lambda i, j: (i, jb + j))],
        out_specs=pl.BlockSpec((tm, tn), lambda i, j: (i, j)),
        out_shape=jax.ShapeDtypeStruct((M, N), BF16), compiler_params=_params(2), name="gated_merge",
    )(attn, gm, w_pa, w_pb, proj, proj)


def _prep_kernel(*refs, heads, hd, rope, qscale):
    if rope:
        q_ref, k_ref, v_ref, cos_ref, sin_ref, qT_ref, ko_ref, vT_ref = refs
        cos, sin = cos_ref[...], sin_ref[...]
        lane = lax.broadcasted_iota(jnp.int32, cos.shape, 1)
        first = (lane % 32) < 16

        def rot(t):
            partner = jnp.where(first, pltpu.roll(t, hd - 16, 1), pltpu.roll(t, 16, 1))
            return t * cos + partner * sin
    else:
        q_ref, k_ref, v_ref, qT_ref, ko_ref, vT_ref = refs

        def rot(t):
            return t
    for h in range(heads):
        sl = slice(h * hd, (h + 1) * hd)
        q = rot(q_ref[:, sl].astype(F32)) * qscale
        qT_ref[sl, :] = q.T.astype(BF16)
        ko_ref[:, sl] = rot(k_ref[:, sl].astype(F32)).astype(BF16)
        vT_ref[sl, :] = v_ref[:, sl].astype(F32).T.astype(BF16)


def _prep(proj, width, hd, qscale, tables=None):
    T = proj.shape[0]
    tr = min(256, T)
    heads = width // hd
    col = lambda c: pl.BlockSpec((tr, width), lambda i: (i, c))
    args, in_specs = [proj, proj, proj], [col(0), col(1), col(2)]
    if tables is not None:
        args += list(tables)
        in_specs += [pl.BlockSpec((tr, hd), lambda i: (i, 0))] * 2
    tspec = pl.BlockSpec((width, tr), lambda i: (0, i))
    return pl.pallas_call(
        functools.partial(_prep_kernel, heads=heads, hd=hd, rope=tables is not None, qscale=qscale),
        grid=(T // tr,), in_specs=in_specs,
        out_specs=[tspec, pl.BlockSpec((tr, width), lambda i: (i, 0)), tspec],
        out_shape=[jax.ShapeDtypeStruct((width, T), BF16), jax.ShapeDtypeStruct((T, width), BF16),
                   jax.ShapeDtypeStruct((width, T), BF16)],
        compiler_params=_params(1), name="prep")(*args)


def _rope_tables(n_tokens, hd):
    freqs = hd // 8
    t = jnp.arange(n_tokens)
    inv = 1.0 / (ROPE_THETA ** (jnp.arange(freqs, dtype=F32) / freqs))
    ar = (t // GRID_W).astype(F32)[:, None] * inv
    ac = (t % GRID_W).astype(F32)[:, None] * inv
    cos = jnp.concatenate([jnp.cos(ar), jnp.cos(ar), jnp.cos(ac), jnp.cos(ac)] * 2, axis=1)
    sin = jnp.concatenate([-jnp.sin(ar), jnp.sin(ar), -jnp.sin(ac), jnp.sin(ac)] * 2, axis=1)
    return cos, sin


def _attn_kernel(lam_ref, qT_ref, k_ref, vT_ref, sub_ref, o_ref,
                 qbd_ref, s_ref, mt_ref, m_ref, l_ref, acc_ref, *, n, tq, dk, out_scale):
    j = pl.program_id(2)
    W = 2 * tq

    n_groups = W // MXU_COLS

    def scores(slot, c):
        cols = slice(c * MXU_COLS, (c + 1) * MXU_COLS)
        s = _dot(k_ref[...], qbd_ref[:, cols])
        s_ref[slot, :, cols] = s
        mt_ref[slot, :, cols] = jnp.max(s, axis=0, keepdims=True)

    def consume(slot, c):
        cols = slice(c * MXU_COLS, (c + 1) * MXU_COLS)
        m_prev = m_ref[:, cols]
        m_new = jnp.maximum(m_prev, mt_ref[slot, :, cols])
        alpha = jnp.exp2(m_prev - m_new)
        p = jnp.exp2(s_ref[slot, :, cols] - m_new)
        l_ref[:, cols] = alpha * l_ref[:, cols] + jnp.sum(p, axis=0, keepdims=True)
        acc_ref[:, cols] = alpha * acc_ref[:, cols] + _dot(vT_ref[...], p.astype(BF16))
        m_ref[:, cols] = m_new

    @pl.when(j == 0)
    def _first():
        z = jnp.zeros((dk, tq), BF16)
        qbd_ref[0:dk, 0:tq] = qT_ref[0:dk, :]
        qbd_ref[0:dk, tq:W] = z
        qbd_ref[dk:2 * dk, 0:tq] = z
        qbd_ref[dk:2 * dk, tq:W] = qT_ref[dk:2 * dk, :]
        m_ref[...] = jnp.full(m_ref.shape, -jnp.inf, F32)
        l_ref[...] = jnp.zeros(l_ref.shape, F32)
        acc_ref[...] = jnp.zeros(acc_ref.shape, F32)
        for c in range(n_groups):
            scores(0, c)

    for parity in range(2):
        @pl.when((j > 0) & (j < n) & (j % 2 == parity))
        def _steady():
            for c in range(n_groups):
                scores(parity, c)
                consume(1 - parity, c)

    @pl.when(j == n)
    def _last():
        for c in range(n_groups):
            consume((n - 1) % 2, c)
        o_both = acc_ref[...] / l_ref[...]
        o = o_both[:, 0:tq] - lam_ref[0] * o_both[:, tq:W]
        y = o * lax.rsqrt(jnp.mean(o * o, axis=0, keepdims=True) + NORM_EPS)
        y = y * (sub_ref[...] * out_scale)
        o_ref[...] = y.T.astype(o_ref.dtype)


def _attention(lam, qT, k, vT, subln, out_scale, tq=1024, tk=1280):
    W, Sq = qT.shape
    hd = subln.shape[0]
    H = W // hd
    T = k.shape[0]
    tq, tk = _tile(Sq, tq), _tile(T, tk)
    n = T // tk
    return pl.pallas_call(
        functools.partial(_attn_kernel, n=n, tq=tq, dk=hd // 2, out_scale=out_scale),
        grid=(H, Sq // tq, n + 1),
        in_specs=[pl.BlockSpec(memory_space=pltpu.SMEM),
                  pl.BlockSpec((hd, tq), lambda h, i, j: (h, i)),
                  pl.BlockSpec((tk, hd), lambda h, i, j: (jnp.minimum(j, n - 1), h)),
                  pl.BlockSpec((hd, tk), lambda h, i, j: (h, jnp.maximum(j - 1, 0))),
                  pl.BlockSpec((hd, 1), lambda h, i, j: (0, 0))],
        out_specs=pl.BlockSpec((tq, hd), lambda h, i, j: (i, h)),
        out_shape=jax.ShapeDtypeStruct((Sq, W), BF16),
        scratch_shapes=[pltpu.VMEM((hd, 2 * tq), BF16), pltpu.VMEM((2, tk, 2 * tq), F32),
                        pltpu.VMEM((2, 1, 2 * tq), F32), pltpu.VMEM((1, 2 * tq), F32),
                        pltpu.VMEM((1, 2 * tq), F32), pltpu.VMEM((hd, 2 * tq), F32)],
        compiler_params=_params(3), name="diff_attention")(lam, qT, k, vT, subln.reshape(hd, 1))


def _gmlp_kernel(u_ref, v_ref, gw_ref, gb_ref, ws_ref, bs_ref, o_ref, *, groups, chunk, gdim):
    v = v_ref[...].astype(F32)
    vc = v - jnp.mean(v, axis=-1, keepdims=True)
    vn = vc * lax.rsqrt(jnp.mean(vc * vc, axis=-1, keepdims=True) + NORM_EPS)
    vn = (vn * gw_ref[...] + gb_ref[...]).astype(BF16)
    n_chunks = v.shape[0] // chunk
    for g in range(groups):
        cols = slice(g * gdim, (g + 1) * gdim)
        rhs = jnp.concatenate([vn[c * chunk:(c + 1) * chunk, cols] for c in range(n_chunks)], axis=1)
        mixed = _dot(ws_ref[g], rhs)
        for c in range(n_chunks):
            rows = slice(c * chunk, (c + 1) * chunk)
            m = mixed[:, c * gdim:(c + 1) * gdim] + bs_ref[g]
            o_ref[rows, cols] = (u_ref[rows, cols].astype(F32) * m).astype(o_ref.dtype)


def _spatial_gating(proj, u_col, v_col, gn_w, gn_b, w_s, b_s):
    T = proj.shape[0]
    G, chunk, _ = w_s.shape
    GW = gn_w.shape[0]
    gdim = GW // G
    tr = min(512, T)
    bias = jnp.broadcast_to(b_s[:, :, None], (G, chunk, gdim)).astype(F32)
    vec = pl.BlockSpec((1, GW), lambda i: (0, 0))
    return pl.pallas_call(
        functools.partial(_gmlp_kernel, groups=G, chunk=chunk, gdim=gdim),
        grid=(T // tr,),
        in_specs=[pl.BlockSpec((tr, GW), lambda i: (i, u_col // GW)),
                  pl.BlockSpec((tr, GW), lambda i: (i, v_col // GW)),
                  vec, vec,
                  pl.BlockSpec((G, chunk, chunk), lambda i: (0, 0, 0)),
                  pl.BlockSpec((G, chunk, gdim), lambda i: (0, 0, 0))],
        out_specs=pl.BlockSpec((tr, GW), lambda i: (i, 0)),
        out_shape=jax.ShapeDtypeStruct((T, GW), BF16),
        compiler_params=_params(1), name="spatial_gating",
    )(proj, proj, gn_w.reshape(1, GW), gn_b.reshape(1, GW), w_s.astype(BF16), bias)


def _gather_rows_kernel(src_ref, nrows_ref, h_hbm, o_ref, sem, *, tg):
    base = pl.program_id(0) * tg

    @pl.when(base < nrows_ref[0])
    def _():
        def start(r, carry):
            pltpu.make_async_copy(h_hbm.at[pl.ds(src_ref[base + r], 1)], o_ref.at[pl.ds(r, 1)], sem).start()
            return carry

        def wait(r, carry):
            pltpu.make_async_copy(h_hbm.at[pl.ds(0, 1)], o_ref.at[pl.ds(r, 1)], sem).wait()
            return carry

        lax.fori_loop(0, tg, start, 0)
        lax.fori_loop(0, tg, wait, 0)

    @pl.when(base >= nrows_ref[0])
    def _():
        o_ref[...] = jnp.zeros(o_ref.shape, o_ref.dtype)


def _gather_rows(h, src, n_rows, R, tg=256):
    W = h.shape[1]
    return pl.pallas_call(
        functools.partial(_gather_rows_kernel, tg=tg),
        grid_spec=pltpu.PrefetchScalarGridSpec(
            num_scalar_prefetch=2, grid=(R // tg,),
            in_specs=[pl.BlockSpec(memory_space=pl.ANY)],
            out_specs=pl.BlockSpec((tg, W), lambda i, src, n: (i, 0)),
            scratch_shapes=[pltpu.SemaphoreType.DMA(())]),
        out_shape=jax.ShapeDtypeStruct((R, W), h.dtype),
        compiler_params=_params(1), name="moe_gather")(src, n_rows, h)


def _unpack_bf16_pairs(w):
    lo = pltpu.bitcast(w << 16, F32).astype(BF16)
    hi = pltpu.bitcast(w & jnp.uint32(0xFFFF0000), F32).astype(BF16)
    return jnp.concatenate([lo, hi], axis=1)


def _moe_up_kernel(te_ref, nt_ref, x_ref, wg_ref, wu_ref, o_ref, xb_ref):
    t, j = pl.program_id(0), pl.program_id(1)

    @pl.when(t < nt_ref[0])
    def _():
        @pl.when(j == 0)
        def _():
            xb_ref[...] = _unpack_bf16_pairs(x_ref[...])

        a = xb_ref[...]
        g = _dot(a, wg_ref[0])
        u = _dot(a, wu_ref[0])
        o_ref[...] = (g * jax.nn.sigmoid(g) * u).astype(o_ref.dtype)

    @pl.when(t >= nt_ref[0])
    def _():
        o_ref[...] = jnp.zeros(o_ref.shape, o_ref.dtype)


def _moe_down_kernel(te_ref, nt_ref, a_ref, w_ref, o_ref):
    @pl.when(pl.program_id(0) < nt_ref[0])
    def _():
        o_ref[...] = _dot(a_ref[...], w_ref[0]).astype(o_ref.dtype)

    @pl.when(pl.program_id(0) >= nt_ref[0])
    def _():
        o_ref[...] = jnp.zeros(o_ref.shape, o_ref.dtype)


def _moe_experts(xs, tile_expert, n_tiles, wg, wu, wd, tm, tf=512, tn=1024):
    R = xs.shape[0]
    E, D, F = wg.shape
    NT = R // tm
    tf, tn = _tile(F, tf), _tile(D, tn)
    tile = lambda t, nt: jnp.minimum(t, nt[0] - 1)

    def wmap(last_j):
        return lambda t, j, te, nt: (te[t], 0, jnp.where(t < nt[0], j, last_j))

    hidden = pl.pallas_call(
        _moe_up_kernel,
        grid_spec=pltpu.PrefetchScalarGridSpec(
            num_scalar_prefetch=2, grid=(NT, F // tf),
            in_specs=[pl.BlockSpec((tm, D // 2), lambda t, j, te, nt: (tile(t, nt), 0)),
                      pl.BlockSpec((1, D, tf), wmap(F // tf - 1)),
                      pl.BlockSpec((1, D, tf), wmap(F // tf - 1))],
            out_specs=pl.BlockSpec((tm, tf), lambda t, j, te, nt: (t, j)),
            scratch_shapes=[pltpu.VMEM((tm, D), BF16)]),
        out_shape=jax.ShapeDtypeStruct((R, F), BF16),
        compiler_params=_params(2), name="moe_up")(tile_expert, n_tiles, xs, wg, wu)
    return pl.pallas_call(
        _moe_down_kernel,
        grid_spec=pltpu.PrefetchScalarGridSpec(
            num_scalar_prefetch=2, grid=(NT, D // tn),
            in_specs=[pl.BlockSpec((tm, F), lambda t, j, te, nt: (tile(t, nt), 0)),
                      pl.BlockSpec((1, F, tn), wmap(D // tn - 1))],
            out_specs=pl.BlockSpec((tm, tn), lambda t, j, te, nt: (t, j))),
        out_shape=jax.ShapeDtypeStruct((R, D), F32),
        compiler_params=_params(2), name="moe_down")(tile_expert, n_tiles, hidden, wd)


def _moe_combine_kernel(pos_ref, ys_hbm, x_ref, wt_ref, pw_ref, g_ref, o_ref, buf_ref, sem, *, tr):
    base = pl.program_id(0) * tr * TOP_K

    def start(r, carry):
        for kk in range(TOP_K):
            pltpu.make_async_copy(ys_hbm.at[pl.ds(pos_ref[base + TOP_K * r + kk], 1)],
                                  buf_ref.at[kk, pl.ds(r, 1)], sem).start()
        return carry

    def wait(r, carry):
        for kk in range(TOP_K):
            pltpu.make_async_copy(ys_hbm.at[pl.ds(0, 1)], buf_ref.at[kk, pl.ds(r, 1)], sem).wait()
        return carry

    lax.fori_loop(0, tr, start, 0)
    lax.fori_loop(0, tr, wait, 0)
    wt = wt_ref[...]
    f = wt[:, 2:3] * buf_ref[0] + wt[:, 3:4] * buf_ref[1]
    fn = f * lax.rsqrt(jnp.mean(f * f, axis=-1, keepdims=True) + NORM_EPS)
    o_ref[...] = x_ref[...] + g_ref[...] * (fn * pw_ref[...])


def _moe_combine(pos, ys, x, route, post_w, gate, tr=128):
    T, D = x.shape
    tr = min(tr, T)
    vec = pl.BlockSpec((1, D), lambda i, p: (0, 0))
    return pl.pallas_call(
        functools.partial(_moe_combine_kernel, tr=tr),
        grid_spec=pltpu.PrefetchScalarGridSpec(
            num_scalar_prefetch=1, grid=(T // tr,),
            in_specs=[pl.BlockSpec(memory_space=pl.ANY),
                      pl.BlockSpec((tr, D), lambda i, p: (i, 0)),
                      pl.BlockSpec((tr, LANES), lambda i, p: (i, 0)),
                      vec, vec],
            out_specs=pl.BlockSpec((tr, D), lambda i, p: (i, 0)),
            scratch_shapes=[pltpu.VMEM((TOP_K, tr, D), F32), pltpu.SemaphoreType.DMA(())]),
        out_shape=jax.ShapeDtypeStruct((T, D), F32),
        compiler_params=_params(1), name="moe_combine",
    )(pos, ys, x, route, post_w.reshape(1, D), gate.reshape(1, D))


def _moe_layer(x, h_packed, route, wg, wu, wd, post_w, gate, tm=512):
    T = x.shape[0]
    E = wg.shape[0]
    tm = min(tm, T)
    A = T * TOP_K
    R = _round_up(A, tm) + E * tm
    expert = route[:, :TOP_K].astype(jnp.int32).reshape(A)
    onehot = (expert[:, None] == jnp.arange(E, dtype=jnp.int32)[None, :]).astype(jnp.int32)
    csum = jnp.cumsum(onehot, axis=0)
    counts = csum[-1]
    rank = jnp.sum((csum - 1) * onehot, axis=1)
    padded = (counts + tm - 1) // tm * tm
    ends = jnp.cumsum(padded)
    offsets = ends - padded
    pos = (jnp.sum(onehot * offsets[None, :], axis=1) + rank).astype(jnp.int32)
    src = jnp.zeros((R,), jnp.int32).at[pos].set(jnp.arange(A, dtype=jnp.int32) // TOP_K)
    n_rows = ends[-1:].astype(jnp.int32)
    n_tiles = n_rows // tm
    tile_start = jnp.arange(R // tm, dtype=jnp.int32) * tm
    tile_expert = jnp.sum((tile_start[:, None] >= ends[None, :]).astype(jnp.int32), axis=1)
    tile_expert = jnp.minimum(tile_expert, jnp.max(jnp.where(counts > 0, jnp.arange(E), 0))).astype(jnp.int32)
    xs = _gather_rows(h_packed, src, n_rows, R)
    ys = _moe_experts(xs, tile_expert, n_tiles, wg, wu, wd, tm)
    return _moe_combine(pos, ys, x, route, post_w, gate)


def _token_mixer(h, p, qscale, tables):
    proj = _matmul(h, p["w_in"], BF16)
    qT, k, vT = _prep(proj, p["qk_w"], p["hd"], qscale, tables)
    return proj, qT, k, vT


def kernel(x, c, ctx, c_ctx, w_ada, b_ada, pre_norm_mix, post_norm_mix, pre_norm_ffn, post_norm_ffn, w_in, lambda_q1, lambda_k1, lambda_q2, lambda_k2, da_subln, gm_norm_w, gm_norm_b, gm_w_s, gm_b_s, w_branch_attn, w_branch_gmlp, w_out, ffn_w_gate, ffn_w_up, ffn_w_down, moe_w_router, moe_b_router, moe_w_gate, moe_w_up, moe_w_down):
    B, S, D = x.shape
    assert B == 1, "the kernels process one sequence"
    L = w_ada.shape[0]
    hd = da_subln.shape[1]
    dk = lambda_q1.shape[1]
    assert hd == 2 * dk == LANES
    qk_w = w_branch_attn.shape[1]
    gm_w = gm_norm_w.shape[1]
    u_col, v_col = 3 * qk_w, 3 * qk_w + gm_w
    ga_col, gb_col = 3 * qk_w + 2 * gm_w, 3 * qk_w + 2 * gm_w + D
    assert w_in.shape[2] == gb_col + D
    qscale = dk ** -0.5 * LOG2E

    xl, xc = x[0], ctx[0]
    cc = jnp.zeros((8, D), F32).at[0].set(c[0]).at[1].set(c_ctx)
    mod = _ada(cc, w_ada, b_ada)
    mod_l = mod[:, 0].reshape(L, 6, D)
    mod_c = mod[:, 1].reshape(L, 6, D)
    tables = _rope_tables(S, hd)

    hl = _norm_call(xl, pre=(pre_norm_mix[0], mod_l[0, 0], mod_l[0, 1]))[0]
    hc = _norm_call(xc, pre=(pre_norm_mix[0], mod_c[0, 0], mod_c[0, 1]))[0]

    for l in range(L):
        need_ctx = l < L - 1
        last = l == L - 1
        lambda_init = 0.8 - 0.6 * math.exp(-0.3 * l)
        lam = (jnp.exp(jnp.sum(lambda_q1[l] * lambda_k1[l])) - jnp.exp(jnp.sum(lambda_q2[l] * lambda_k2[l]))
               + lambda_init).reshape(1).astype(F32)
        p = {"w_in": w_in[l].astype(BF16), "qk_w": qk_w, "hd": hd}
        w_pa, w_pb, w_o = w_branch_attn[l].astype(BF16), w_branch_gmlp[l].astype(BF16), w_out[l].astype(BF16)
        out_scale = 1.0 - lambda_init
        is_moe = l % 2 == 1
        i = l // 2

        proj_l, qT_l, k_l, vT_l = _token_mixer(hl, p, qscale, tables)
        proj_c, qT_c, k_c, vT_c = _token_mixer(hc, p, qscale, None)
        k_all = jnp.concatenate([k_l, k_c], axis=0)
        vT_all = jnp.concatenate([vT_l, vT_c], axis=1)
        attn_l = _attention(lam, qT_l, k_all, vT_all, da_subln[l], out_scale)
        gm_l = _spatial_gating(proj_l, u_col, v_col, gm_norm_w[l], gm_norm_b[l], gm_w_s[l], gm_b_s[l])
        ol = _matmul(_gated_merge(attn_l, gm_l, proj_l, ga_col, gb_col, w_pa, w_pb), w_o, F32)
        if need_ctx:
            attn_c = _attention(lam, qT_c, k_c, vT_c, da_subln[l], out_scale)
            gm_c = _spatial_gating(proj_c, u_col, v_col, gm_norm_w[l], gm_norm_b[l], gm_w_s[l], gm_b_s[l])
            oc = _matmul(_gated_merge(attn_c, gm_c, proj_c, ga_col, gb_col, w_pa, w_pb), w_o, F32)

        def mixer_update(xr, o, m):
            router = (moe_w_router[i], moe_b_router[i]) if is_moe else None
            return _norm_call(xr, update=(o, post_norm_mix[l], m[l, 2]),
                              pre=(pre_norm_ffn[l], m[l, 3], m[l, 4]), router=router, pack_h=is_moe)

        def next_pre(m):
            return None if last else (pre_norm_mix[l + 1], m[l + 1, 0], m[l + 1, 1])

        if is_moe:
            wg, wu, wd = moe_w_gate[i].astype(BF16), moe_w_up[i].astype(BF16), moe_w_down[i].astype(BF16)
            xl, hl2, route = mixer_update(xl, ol, mod_l)
            xl = _moe_layer(xl, hl2, route, wg, wu, wd, post_norm_ffn[l], mod_l[l, 5])
            if not last:
                hl = _norm_call(xl, pre=next_pre(mod_l))[0]
            if need_ctx:
                xc, hc2, route_c = mixer_update(xc, oc, mod_c)
                xc = _moe_layer(xc, hc2, route_c, wg, wu, wd, post_norm_ffn[l], mod_c[l, 5])
                hc = _norm_call(xc, pre=next_pre(mod_c))[0]
        else:
            FF = ffn_w_gate.shape[2]
            FFp = _round_up(FF, 1024)
            wg = jnp.zeros((D, FFp), BF16).at[:, :FF].set(ffn_w_gate[i].astype(BF16))
            wu = jnp.zeros((D, FFp), BF16).at[:, :FF].set(ffn_w_up[i].astype(BF16))
            wd = jnp.zeros((FFp, D), BF16).at[:FF].set(ffn_w_down[i].astype(BF16))

            def dense(xr, o, m):
                xr, h2 = mixer_update(xr, o, m)
                f = _matmul(_swiglu_up(h2, wg, wu), wd, F32)
                res = _norm_call(xr, update=(f, post_norm_ffn[l], m[l, 5]), pre=next_pre(m))
                return res[0], (res[1] if not last else None)

            xl, hl = dense(xl, ol, mod_l)
            if need_ctx:
                xc, hc = dense(xc, oc, mod_c)
            elif not last:
                hc = _norm_call(xc, pre=next_pre(mod_c))[0]
    return xl[None]
```

```python
import functools
import math

import jax
import jax.numpy as jnp
from jax import lax
from jax.experimental import pallas as pl
from jax.experimental.pallas import tpu as pltpu

F32 = jnp.float32
BF16 = jnp.bfloat16
F8 = jnp.float8_e4m3fn

GRID_W = 64
ROPE_THETA = 10000.0
NORM_EPS = 1e-6
TOP_K = 2
LANES = 128
MXU_COLS = 256
ONES_ROWS = 16
VMEM_LIMIT = 56 * 1024 * 1024
LOG2E = 1.4426950408889634


def _params(n_axes):
    return pltpu.CompilerParams(dimension_semantics=("arbitrary",) * n_axes,
                                vmem_limit_bytes=VMEM_LIMIT)


def _dot(a, b):
    return jnp.dot(a, b, preferred_element_type=F32)


def _round_up(n, m):
    return (n + m - 1) // m * m


def _tile(n, pref):
    if n <= pref:
        return n
    t = pref - pref % LANES
    while n % t:
        t -= LANES
    return t


def _ada_kernel(c_ref, w_ref, b_ref, o_ref):
    c = c_ref[...]
    s = (c * jax.nn.sigmoid(c)).astype(BF16)
    o_ref[0] = _dot(s, w_ref[0].astype(BF16)) + b_ref[0]


def _ada(cc, w_ada, b_ada):
    L, D, N = w_ada.shape
    R = cc.shape[0]
    tn = _tile(N, 512)
    return pl.pallas_call(
        _ada_kernel,
        grid=(L, N // tn),
        in_specs=[pl.BlockSpec((R, D), lambda l, j: (0, 0)),
                  pl.BlockSpec((1, D, tn), lambda l, j: (l, 0, j)),
                  pl.BlockSpec((1, 1, tn), lambda l, j: (l, 0, j))],
        out_specs=pl.BlockSpec((1, R, tn), lambda l, j: (l, 0, j)),
        out_shape=jax.ShapeDtypeStruct((L, R, N), F32),
        compiler_params=_params(2),
        name="ada",
    )(cc, w_ada, b_ada.reshape(L, 1, N))


def _top2_rows(logits, n_experts):
    lane = lax.broadcasted_iota(jnp.int32, logits.shape, 1)
    lane_f = lane.astype(F32)
    neg = jnp.float32(-jnp.inf)
    lg = jnp.where(lane < n_experts, logits, neg)
    v1 = jnp.max(lg, axis=-1, keepdims=True)
    i1 = jnp.min(jnp.where(lg == v1, lane_f, float(LANES)), axis=-1, keepdims=True)
    lg2 = jnp.where(lane_f == i1, neg, lg)
    v2 = jnp.max(lg2, axis=-1, keepdims=True)
    i2 = jnp.min(jnp.where(lg2 == v2, lane_f, float(LANES)), axis=-1, keepdims=True)
    e = jnp.exp(v2 - v1)
    w1 = 1.0 / (1.0 + e)
    w2 = e / (1.0 + e)
    return jnp.where(lane == 0, i1, jnp.where(lane == 1, i2, jnp.where(lane == 2, w1, jnp.where(lane == 3, w2, 0.0))))


def _norm_kernel(*refs, has_update, has_pre, pack_h, n_experts):
    it = iter(refs)
    x_ref = next(it)
    if has_update:
        u_ref, pw_ref, g_ref = next(it), next(it), next(it)
    if has_pre:
        w_ref, sh_ref, sc_ref = next(it), next(it), next(it)
    if n_experts:
        wr_ref, br_ref = next(it), next(it)
    if has_update:
        xo_ref = next(it)
    if has_pre:
        h_ref = next(it)
    if n_experts:
        r_ref = next(it)

    x = x_ref[...]
    if has_update:
        u = u_ref[...].astype(F32)
        un = u * lax.rsqrt(jnp.mean(u * u, axis=-1, keepdims=True) + NORM_EPS)
        x = x + g_ref[...] * (un * pw_ref[...])
        xo_ref[...] = x
    if has_pre:
        y = x * lax.rsqrt(jnp.mean(x * x, axis=-1, keepdims=True) + NORM_EPS)
        h = (y * w_ref[...]) * (1.0 + sc_ref[...]) + sh_ref[...]
        if pack_h:
            half = h.shape[1] // 2
            lo = pltpu.bitcast(h[:, :half].astype(BF16).astype(F32), jnp.uint32)
            hi = pltpu.bitcast(h[:, half:].astype(BF16).astype(F32), jnp.uint32)
            h_ref[...] = (lo >> 16) | (hi & jnp.uint32(0xFFFF0000))
        else:
            h_ref[...] = h.astype(BF16)
        if n_experts:
            wr = wr_ref[...]
            h_hi = h.astype(BF16)
            h_lo = (h - h_hi.astype(F32)).astype(BF16)
            w_hi = wr.astype(BF16)
            w_lo = (wr - w_hi.astype(F32)).astype(BF16)
            logits = _dot(h_hi, w_hi) + (_dot(h_hi, w_lo) + _dot(h_lo, w_hi)) + br_ref[...]
            r_ref[...] = _top2_rows(logits, n_experts)


def _norm_call(x, update=None, pre=None, router=None, pack_h=False):
    T, D = x.shape
    tr = min(256, T)
    row = pl.BlockSpec((tr, D), lambda i: (i, 0))
    vec = pl.BlockSpec((1, D), lambda i: (0, 0))
    args, in_specs, out_shape, out_specs = [x], [row], [], []
    if update is not None:
        upd, pw, g = update
        args += [upd, pw.reshape(1, D), g.reshape(1, D)]
        in_specs += [row, vec, vec]
        out_shape.append(jax.ShapeDtypeStruct((T, D), F32))
        out_specs.append(row)
    if pre is not None:
        w, sh, sc = pre
        args += [w.reshape(1, D), sh.reshape(1, D), sc.reshape(1, D)]
        in_specs += [vec, vec, vec]
        if pack_h:
            out_shape.append(jax.ShapeDtypeStruct((T, D // 2), jnp.uint32))
            out_specs.append(pl.BlockSpec((tr, D // 2), lambda i: (i, 0)))
        else:
            out_shape.append(jax.ShapeDtypeStruct((T, D), BF16))
            out_specs.append(row)
    n_experts = 0
    if router is not None:
        wr, br = router
        n_experts = wr.shape[1]
        wr_p = jnp.zeros((D, LANES), F32).at[:, :n_experts].set(wr)
        br_p = jnp.zeros((1, LANES), F32).at[0, :n_experts].set(br)
        args += [wr_p, br_p]
        in_specs += [pl.BlockSpec((D, LANES), lambda i: (0, 0)), pl.BlockSpec((1, LANES), lambda i: (0, 0))]
        out_shape.append(jax.ShapeDtypeStruct((T, LANES), F32))
        out_specs.append(pl.BlockSpec((tr, LANES), lambda i: (i, 0)))
    kern = functools.partial(_norm_kernel, has_update=update is not None, has_pre=pre is not None,
                             pack_h=pack_h, n_experts=n_experts)
    return pl.pallas_call(kern, grid=(T // tr,), in_specs=in_specs, out_specs=out_specs,
                          out_shape=out_shape, compiler_params=_params(1), name="norm")(*args)


def _mm_kernel(a_ref, b_ref, o_ref):
    o_ref[...] = _dot(a_ref[...], b_ref[...]).astype(o_ref.dtype)


def _mm_acc_kernel(a_ref, b_ref, o_ref, acc_ref, *, nk):
    k = pl.program_id(2)
    p = _dot(a_ref[...], b_ref[...])

    @pl.when(k == 0)
    def _():
        acc_ref[...] = p

    @pl.when(k > 0)
    def _():
        acc_ref[...] += p

    @pl.when(k == nk - 1)
    def _():
        o_ref[...] = acc_ref[...].astype(o_ref.dtype)


def _k_tile(K, limit=4096):
    if K <= limit:
        return K
    tk = limit - limit % 256
    while K % tk:
        tk -= 256
    return tk


def _matmul(a, b, out_dtype, tm=1024, tn=1024):
    M, K = a.shape
    N = b.shape[1]
    tm, tn, tk = _tile(M, tm), _tile(N, tn), _k_tile(K)
    out_shape = jax.ShapeDtypeStruct((M, N), out_dtype)
    if tk == K:
        return pl.pallas_call(
            _mm_kernel, grid=(M // tm, N // tn),
            in_specs=[pl.BlockSpec((tm, K), lambda i, j: (i, 0)), pl.BlockSpec((K, tn), lambda i, j: (0, j))],
            out_specs=pl.BlockSpec((tm, tn), lambda i, j: (i, j)),
            out_shape=out_shape, compiler_params=_params(2), name="matmul")(a, b)
    nk = K // tk
    return pl.pallas_call(
        functools.partial(_mm_acc_kernel, nk=nk), grid=(M // tm, N // tn, nk),
        in_specs=[pl.BlockSpec((tm, tk), lambda i, j, k: (i, k)), pl.BlockSpec((tk, tn), lambda i, j, k: (k, j))],
        out_specs=pl.BlockSpec((tm, tn), lambda i, j, k: (i, j)),
        out_shape=out_shape, scratch_shapes=[pltpu.VMEM((tm, tn), F32)],
        compiler_params=_params(3), name="matmul_acc")(a, b)


def _swiglu_kernel(a_ref, wg_ref, wu_ref, o_ref):
    a = a_ref[...]
    g = _dot(a, wg_ref[...])
    u = _dot(a, wu_ref[...])
    o_ref[...] = (g * jax.nn.sigmoid(g) * u).astype(o_ref.dtype)


def _swiglu_up(a, wg, wu, tm=1024, tn=512):
    M, K = a.shape
    N = wg.shape[1]
    tm, tn = _tile(M, tm), _tile(N, tn)
    wspec = pl.BlockSpec((K, tn), lambda i, j: (0, j))
    return pl.pallas_call(
        _swiglu_kernel, grid=(M // tm, N // tn),
        in_specs=[pl.BlockSpec((tm, K), lambda i, j: (i, 0)), wspec, wspec],
        out_specs=pl.BlockSpec((tm, tn), lambda i, j: (i, j)),
        out_shape=jax.ShapeDtypeStruct((M, N), BF16), compiler_params=_params(2), name="swiglu_up")(a, wg, wu)


def _merge_kernel(a1_ref, a2_ref, w1_ref, w2_ref, g1_ref, g2_ref, o_ref):
    y1 = _dot(a1_ref[...], w1_ref[...])
    y2 = _dot(a2_ref[...], w2_ref[...])
    g1 = jax.nn.sigmoid(g1_ref[...].astype(F32))
    g2 = jax.nn.sigmoid(g2_ref[...].astype(F32))
    o_ref[...] = (g1 * y1 + g2 * y2).astype(o_ref.dtype)


def _gated_merge(attn, gm, proj, ga_col, gb_col, w_pa, w_pb, tm=1024, tn=512):
    M, K1 = attn.shape
    K2 = gm.shape[1]
    N = w_pa.shape[1]
    tm, tn = _tile(M, tm), _tile(N, tn)
    ja, jb = ga_col // tn, gb_col // tn
    return pl.pallas_call(
        _merge_kernel, grid=(M // tm, N // tn),
        in_specs=[pl.BlockSpec((tm, K1), lambda i, j: (i, 0)),
                  pl.BlockSpec((tm, K2), lambda i, j: (i, 0)),
                  pl.BlockSpec((K1, tn), lambda i, j: (0, j)),
                  pl.BlockSpec((K2, tn), lambda i, j: (0, j)),
                  pl.BlockSpec((tm, tn), lambda i, j: (i, ja + j)),
                  pl.BlockSpec((tm, tn), lambda i, j: (i, jb + j))],
        out_specs=pl.BlockSpec((tm, tn), lambda i, j: (i, j)),
        out_shape=jax.ShapeDtypeStruct((M, N), BF16), compiler_params=_params(2), name="gated_merge",
    )(attn, gm, w_pa, w_pb, proj, proj)


def _prep_kernel(*refs, heads, hd, rope, qscale):
    if rope:
        q_ref, k_ref, v_ref, cos_ref, sin_ref, qT_ref, ko_ref, vT_ref, amax_ref = refs
        cos, sin = cos_ref[...], sin_ref[...]
        lane = lax.broadcasted_iota(jnp.int32, cos.shape, 1)
        first = (lane % 32) < 16

        def rot(t):
            partner = jnp.where(first, pltpu.roll(t, hd - 16, 1), pltpu.roll(t, 16, 1))
            return t * cos + partner * sin
    else:
        q_ref, k_ref, v_ref, qT_ref, ko_ref, vT_ref, amax_ref = refs

        def rot(t):
            return t
    hv = hd + ONES_ROWS
    ones = jnp.ones((ONES_ROWS, q_ref.shape[0]), BF16)
    qmax = kmax = None
    for h in range(heads):
        sl = slice(h * hd, (h + 1) * hd)
        q = (rot(q_ref[:, sl].astype(F32)) * qscale).astype(BF16)
        k = rot(k_ref[:, sl].astype(F32)).astype(BF16)
        qT_ref[sl, :] = q.astype(F32).T.astype(BF16)
        ko_ref[:, sl] = k
        vT_ref[h * hv:h * hv + hd, :] = v_ref[:, sl].astype(F32).T.astype(BF16)
        vT_ref[h * hv + hd:(h + 1) * hv, :] = ones
        qa, ka = jnp.abs(q.astype(F32)), jnp.abs(k.astype(F32))
        qmax = qa if qmax is None else jnp.maximum(qmax, qa)
        kmax = ka if kmax is None else jnp.maximum(kmax, ka)
    row = lax.broadcasted_iota(jnp.int32, (8, hd), 0)
    amax_ref[0] = jnp.where(row == 0, jnp.max(qmax, axis=0, keepdims=True),
                            jnp.where(row == 1, jnp.max(kmax, axis=0, keepdims=True), 0.0))


def _prep(proj, width, hd, qscale, tables=None):
    T = proj.shape[0]
    tr = min(256, T)
    heads = width // hd
    col = lambda c: pl.BlockSpec((tr, width), lambda i: (i, c))
    args, in_specs = [proj, proj, proj], [col(0), col(1), col(2)]
    if tables is not None:
        args += list(tables)
        in_specs += [pl.BlockSpec((tr, hd), lambda i: (i, 0))] * 2
    vrows = heads * (hd + ONES_ROWS)
    return pl.pallas_call(
        functools.partial(_prep_kernel, heads=heads, hd=hd, rope=tables is not None, qscale=qscale),
        grid=(T // tr,), in_specs=in_specs,
        out_specs=[pl.BlockSpec((width, tr), lambda i: (0, i)), pl.BlockSpec((tr, width), lambda i: (i, 0)),
                   pl.BlockSpec((vrows, tr), lambda i: (0, i)), pl.BlockSpec((1, 8, hd), lambda i: (i, 0, 0))],
        out_shape=[jax.ShapeDtypeStruct((width, T), BF16), jax.ShapeDtypeStruct((T, width), BF16),
                   jax.ShapeDtypeStruct((vrows, T), BF16), jax.ShapeDtypeStruct((T // tr, 8, hd), F32)],
        compiler_params=_params(1), name="prep")(*args)


def _split8_kernel(sc_ref, k_ref, o_ref, *, heads, hd):
    half = hd // 2
    lane = lax.broadcasted_iota(jnp.int32, (k_ref.shape[0], hd), 1)
    low = lane < half
    for h in range(heads):
        x = k_ref[:, h * hd:(h + 1) * hd].astype(F32) * sc_ref[0]
        hi = x.astype(F8).astype(F32)
        lo = (x - hi).astype(F8).astype(F32)
        hi_r, lo_r = pltpu.roll(hi, half, 1), pltpu.roll(lo, half, 1)
        pieces = (jnp.where(low, hi, lo_r), jnp.where(low, hi, 0.0),
                  jnp.where(low, hi_r, lo), jnp.where(low, hi_r, 0.0))
        for i, piece in enumerate(pieces):
            o_ref[:, (4 * h + i) * hd:(4 * h + i + 1) * hd] = piece.astype(F8)


def _split8(k, scale, hd):
    T, width = k.shape
    tr = min(256, T)
    return pl.pallas_call(
        functools.partial(_split8_kernel, heads=width // hd, hd=hd),
        grid=(T // tr,),
        in_specs=[pl.BlockSpec(memory_space=pltpu.SMEM), pl.BlockSpec((tr, width), lambda i: (i, 0))],
        out_specs=pl.BlockSpec((tr, 4 * width), lambda i: (i, 0)),
        out_shape=jax.ShapeDtypeStruct((T, 4 * width), F8),
        compiler_params=_params(1), name="split8")(scale, k)


def _fp8_scales(qmax, kmax):
    def pow2(m):
        e = jnp.floor(jnp.log2(256.0 / jnp.maximum(m, 1e-30)))
        return jnp.clip(e, -60.0, 60.0)
    eq, ek = pow2(qmax), pow2(kmax)
    return jnp.exp2(eq), jnp.exp2(ek), jnp.exp2(-(eq + ek))


def _rope_tables(n_tokens, hd):
    freqs = hd // 8
    t = jnp.arange(n_tokens)
    inv = 1.0 / (ROPE_THETA ** (jnp.arange(freqs, dtype=F32) / freqs))
    ar = (t // GRID_W).astype(F32)[:, None] * inv
    ac = (t % GRID_W).astype(F32)[:, None] * inv
    cos = jnp.concatenate([jnp.cos(ar), jnp.cos(ar), jnp.cos(ac), jnp.cos(ac)] * 2, axis=1)
    sin = jnp.concatenate([-jnp.sin(ar), jnp.sin(ar), -jnp.sin(ac), jnp.sin(ac)] * 2, axis=1)
    return cos, sin


def _attn_kernel(sc_ref, qT_ref, k_ref, vT_ref, sub_ref, o_ref,
                 q8_ref, s_ref, mt_ref, m_ref, acc_ref, *, n, tq, dk, out_scale):
    j = pl.program_id(2)
    W = 2 * tq
    dv = 2 * dk
    kw = 4 * dk
    n_groups = W // MXU_COLS
    per_map = tq // MXU_COLS

    def scores(slot, c):
        cols = slice(c * MXU_COLS, (c + 1) * MXU_COLS)
        mp, qc = c // per_map, c % per_map
        s = _dot(k_ref[:, mp * kw:(mp + 1) * kw],
                 q8_ref[mp, :, qc * MXU_COLS:(qc + 1) * MXU_COLS]).astype(BF16)
        s_ref[slot, :, cols] = s
        mt_ref[slot, :, cols] = jnp.max(s, axis=0, keepdims=True).astype(F32)

    def consume(slot, c):
        cols = slice(c * MXU_COLS, (c + 1) * MXU_COLS)
        unscale = sc_ref[2]
        m_prev = m_ref[:, cols]
        m_new = jnp.maximum(m_prev, mt_ref[slot, :, cols])
        alpha = jnp.exp2((m_prev - m_new) * unscale)
        p = jnp.exp2((s_ref[slot, :, cols] - m_new.astype(BF16)) * unscale.astype(BF16))
        acc_ref[:, cols] = alpha * acc_ref[:, cols] + _dot(vT_ref[...], p)
        m_ref[:, cols] = m_new

    @pl.when(j == 0)
    def _first():
        q = qT_ref[...].astype(F32) * sc_ref[1]
        hi = q.astype(F8)
        lo = (q - hi.astype(F32)).astype(F8)
        for mp in range(2):
            rows = slice(mp * dk, (mp + 1) * dk)
            q8_ref[mp, 0:dk, :] = hi[rows]
            q8_ref[mp, dk:2 * dk, :] = hi[rows]
            q8_ref[mp, 2 * dk:3 * dk, :] = lo[rows]
            q8_ref[mp, 3 * dk:kw, :] = jnp.zeros((dk, tq), F8)
        m_ref[...] = jnp.full(m_ref.shape, -jnp.inf, F32)
        acc_ref[...] = jnp.zeros(acc_ref.shape, F32)
        for c in range(n_groups):
            scores(0, c)

    for parity in range(2):
        @pl.when((j > 0) & (j < n) & (j % 2 == parity))
        def _steady():
            for c in range(n_groups):
                scores(parity, c)
                consume(1 - parity, c)

    @pl.when(j == n)
    def _last():
        for c in range(n_groups):
            consume((n - 1) % 2, c)
        o_both = acc_ref[0:dv, :] / acc_ref[dv:dv + 1, :]
        o = o_both[:, 0:tq] - sc_ref[0] * o_both[:, tq:W]
        y = o * lax.rsqrt(jnp.mean(o * o, axis=0, keepdims=True) + NORM_EPS)
        y = y * (sub_ref[...] * out_scale)
        o_ref[...] = y.T.astype(o_ref.dtype)


def _attention(lam, qT, qmax, k, kmax, vT, subln, out_scale, tq=1024, tk=1280):
    W, Sq = qT.shape
    hd = subln.shape[0]
    hv = hd + ONES_ROWS
    H = W // hd
    T = k.shape[0]
    tq, tk = _tile(Sq, tq), _tile(T, tk)
    n = T // tk
    q_scale, k_scale, unscale = _fp8_scales(qmax, kmax)
    k8 = _split8(k, k_scale.reshape(1).astype(F32), hd)
    scalars = jnp.stack([lam.reshape(()), q_scale, unscale]).astype(F32)
    return pl.pallas_call(
        functools.partial(_attn_kernel, n=n, tq=tq, dk=hd // 2, out_scale=out_scale),
        grid=(H, Sq // tq, n + 1),
        in_specs=[pl.BlockSpec(memory_space=pltpu.SMEM),
                  pl.BlockSpec((hd, tq), lambda h, i, j: (h, i)),
                  pl.BlockSpec((tk, 4 * hd), lambda h, i, j: (jnp.minimum(j, n - 1), h)),
                  pl.BlockSpec((hv, tk), lambda h, i, j: (h, jnp.maximum(j - 1, 0))),
                  pl.BlockSpec((hd, 1), lambda h, i, j: (0, 0))],
        out_specs=pl.BlockSpec((tq, hd), lambda h, i, j: (i, h)),
        out_shape=jax.ShapeDtypeStruct((Sq, W), BF16),
        scratch_shapes=[pltpu.VMEM((2, 2 * hd, tq), F8), pltpu.VMEM((2, tk, 2 * tq), BF16),
                        pltpu.VMEM((2, 1, 2 * tq), F32), pltpu.VMEM((1, 2 * tq), F32),
                        pltpu.VMEM((hv, 2 * tq), F32)],
        compiler_params=_params(3), name="diff_attention")(scalars, qT, k8, vT, subln.reshape(hd, 1))


def _gmlp_kernel(u_ref, v_ref, gw_ref, gb_ref, ws_ref, bs_ref, o_ref, *, groups, chunk, gdim):
    v = v_ref[...].astype(F32)
    vc = v - jnp.mean(v, axis=-1, keepdims=True)
    vn = vc * lax.rsqrt(jnp.mean(vc * vc, axis=-1, keepdims=True) + NORM_EPS)
    vn = (vn * gw_ref[...] + gb_ref[...]).astype(BF16)
    n_chunks = v.shape[0] // chunk
    for g in range(groups):
        cols = slice(g * gdim, (g + 1) * gdim)
        rhs = jnp.concatenate([vn[c * chunk:(c + 1) * chunk, cols] for c in range(n_chunks)], axis=1)
        mixed = _dot(ws_ref[g], rhs)
        for c in range(n_chunks):
            rows = slice(c * chunk, (c + 1) * chunk)
            m = mixed[:, c * gdim:(c + 1) * gdim] + bs_ref[g]
            o_ref[rows, cols] = (u_ref[rows, cols].astype(F32) * m).astype(o_ref.dtype)


def _spatial_gating(proj, u_col, v_col, gn_w, gn_b, w_s, b_s):
    T = proj.shape[0]
    G, chunk, _ = w_s.shape
    GW = gn_w.shape[0]
    gdim = GW // G
    tr = min(512, T)
    bias = jnp.broadcast_to(b_s[:, :, None], (G, chunk, gdim)).astype(F32)
    vec = pl.BlockSpec((1, GW), lambda i: (0, 0))
    return pl.pallas_call(
        functools.partial(_gmlp_kernel, groups=G, chunk=chunk, gdim=gdim),
        grid=(T // tr,),
        in_specs=[pl.BlockSpec((tr, GW), lambda i: (i, u_col // GW)),
                  pl.BlockSpec((tr, GW), lambda i: (i, v_col // GW)),
                  vec, vec,
                  pl.BlockSpec((G, chunk, chunk), lambda i: (0, 0, 0)),
                  pl.BlockSpec((G, chunk, gdim), lambda i: (0, 0, 0))],
        out_specs=pl.BlockSpec((tr, GW), lambda i: (i, 0)),
        out_shape=jax.ShapeDtypeStruct((T, GW), BF16),
        compiler_params=_params(1), name="spatial_gating",
    )(proj, proj, gn_w.reshape(1, GW), gn_b.reshape(1, GW), w_s.astype(BF16), bias)


def _gather_rows_kernel(src_ref, nrows_ref, h_hbm, o_ref, sem, *, tg):
    base = pl.program_id(0) * tg

    @pl.when(base < nrows_ref[0])
    def _():
        def start(r, carry):
            pltpu.make_async_copy(h_hbm.at[pl.ds(src_ref[base + r], 1)], o_ref.at[pl.ds(r, 1)], sem).start()
            return carry

        def wait(r, carry):
            pltpu.make_async_copy(h_hbm.at[pl.ds(0, 1)], o_ref.at[pl.ds(r, 1)], sem).wait()
            return carry

        lax.fori_loop(0, tg, start, 0)
        lax.fori_loop(0, tg, wait, 0)

    @pl.when(base >= nrows_ref[0])
    def _():
        o_ref[...] = jnp.zeros(o_ref.shape, o_ref.dtype)


def _gather_rows(h, src, n_rows, R, tg=256):
    W = h.shape[1]
    return pl.pallas_call(
        functools.partial(_gather_rows_kernel, tg=tg),
        grid_spec=pltpu.PrefetchScalarGridSpec(
            num_scalar_prefetch=2, grid=(R // tg,),
            in_specs=[pl.BlockSpec(memory_space=pl.ANY)],
            out_specs=pl.BlockSpec((tg, W), lambda i, src, n: (i, 0)),
            scratch_shapes=[pltpu.SemaphoreType.DMA(())]),
        out_shape=jax.ShapeDtypeStruct((R, W), h.dtype),
        compiler_params=_params(1), name="moe_gather")(src, n_rows, h)


def _unpack_bf16_pairs(w):
    lo = pltpu.bitcast(w << 16, F32).astype(BF16)
    hi = pltpu.bitcast(w & jnp.uint32(0xFFFF0000), F32).astype(BF16)
    return jnp.concatenate([lo, hi], axis=1)


def _moe_up_kernel(te_ref, nt_ref, x_ref, wg_ref, wu_ref, o_ref, xb_ref):
    t, j = pl.program_id(0), pl.program_id(1)

    @pl.when(t < nt_ref[0])
    def _():
        @pl.when(j == 0)
        def _():
            xb_ref[...] = _unpack_bf16_pairs(x_ref[...])

        a = xb_ref[...]
        g = _dot(a, wg_ref[0])
        u = _dot(a, wu_ref[0])
        o_ref[...] = (g * jax.nn.sigmoid(g) * u).astype(o_ref.dtype)

    @pl.when(t >= nt_ref[0])
    def _():
        o_ref[...] = jnp.zeros(o_ref.shape, o_ref.dtype)


def _moe_down_kernel(te_ref, nt_ref, a_ref, w_ref, o_ref):
    @pl.when(pl.program_id(0) < nt_ref[0])
    def _():
        o_ref[...] = _dot(a_ref[...], w_ref[0]).astype(o_ref.dtype)

    @pl.when(pl.program_id(0) >= nt_ref[0])
    def _():
        o_ref[...] = jnp.zeros(o_ref.shape, o_ref.dtype)


def _moe_experts(xs, tile_expert, n_tiles, wg, wu, wd, tm, tf=512, tn=1024):
    R = xs.shape[0]
    E, D, F = wg.shape
    NT = R // tm
    tf, tn = _tile(F, tf), _tile(D, tn)
    tile = lambda t, nt: jnp.minimum(t, nt[0] - 1)

    def wmap(last_j):
        return lambda t, j, te, nt: (te[t], 0, jnp.where(t < nt[0], j, last_j))

    hidden = pl.pallas_call(
        _moe_up_kernel,
        grid_spec=pltpu.PrefetchScalarGridSpec(
            num_scalar_prefetch=2, grid=(NT, F // tf),
            in_specs=[pl.BlockSpec((tm, D // 2), lambda t, j, te, nt: (tile(t, nt), 0)),
                      pl.BlockSpec((1, D, tf), wmap(F // tf - 1)),
                      pl.BlockSpec((1, D, tf), wmap(F // tf - 1))],
            out_specs=pl.BlockSpec((tm, tf), lambda t, j, te, nt: (t, j)),
            scratch_shapes=[pltpu.VMEM((tm, D), BF16)]),
        out_shape=jax.ShapeDtypeStruct((R, F), BF16),
        compiler_params=_params(2), name="moe_up")(tile_expert, n_tiles, xs, wg, wu)
    return pl.pallas_call(
        _moe_down_kernel,
        grid_spec=pltpu.PrefetchScalarGridSpec(
            num_scalar_prefetch=2, grid=(NT, D // tn),
            in_specs=[pl.BlockSpec((tm, F), lambda t, j, te, nt: (tile(t, nt), 0)),
                      pl.BlockSpec((1, F, tn), wmap(D // tn - 1))],
            out_specs=pl.BlockSpec((tm, tn), lambda t, j, te, nt: (t, j))),
        out_shape=jax.ShapeDtypeStruct((R, D), F32),
        compiler_params=_params(2), name="moe_down")(tile_expert, n_tiles, hidden, wd)


def _moe_combine_kernel(pos_ref, ys_hbm, x_ref, wt_ref, pw_ref, g_ref, o_ref, buf_ref, sem, *, tr):
    base = pl.program_id(0) * tr * TOP_K

    def start(r, carry):
        for kk in range(TOP_K):
            pltpu.make_async_copy(ys_hbm.at[pl.ds(pos_ref[base + TOP_K * r + kk], 1)],
                                  buf_ref.at[kk, pl.ds(r, 1)], sem).start()
        return carry

    def wait(r, carry):
        for kk in range(TOP_K):
            pltpu.make_async_copy(ys_hbm.at[pl.ds(0, 1)], buf_ref.at[kk, pl.ds(r, 1)], sem).wait()
        return carry

    lax.fori_loop(0, tr, start, 0)
    lax.fori_loop(0, tr, wait, 0)
    wt = wt_ref[...]
    f = wt[:, 2:3] * buf_ref[0] + wt[:, 3:4] * buf_ref[1]
    fn = f * lax.rsqrt(jnp.mean(f * f, axis=-1, keepdims=True) + NORM_EPS)
    o_ref[...] = x_ref[...] + g_ref[...] * (fn * pw_ref[...])


def _moe_combine(pos, ys, x, route, post_w, gate, tr=128):
    T, D = x.shape
    tr = min(tr, T)
    vec = pl.BlockSpec((1, D), lambda i, p: (0, 0))
    return pl.pallas_call(
        functools.partial(_moe_combine_kernel, tr=tr),
        grid_spec=pltpu.PrefetchScalarGridSpec(
            num_scalar_prefetch=1, grid=(T // tr,),
            in_specs=[pl.BlockSpec(memory_space=pl.ANY),
                      pl.BlockSpec((tr, D), lambda i, p: (i, 0)),
                      pl.BlockSpec((tr, LANES), lambda i, p: (i, 0)),
                      vec, vec],
            out_specs=pl.BlockSpec((tr, D), lambda i, p: (i, 0)),
            scratch_shapes=[pltpu.VMEM((TOP_K, tr, D), F32), pltpu.SemaphoreType.DMA(())]),
        out_shape=jax.ShapeDtypeStruct((T, D), F32),
        compiler_params=_params(1), name="moe_combine",
    )(pos, ys, x, route, post_w.reshape(1, D), gate.reshape(1, D))


def _moe_layer(x, h_packed, route, wg, wu, wd, post_w, gate, tm=512):
    T = x.shape[0]
    E = wg.shape[0]
    tm = min(tm, T)
    A = T * TOP_K
    R = _round_up(A, tm) + E * tm
    expert = route[:, :TOP_K].astype(jnp.int32).reshape(A)
    onehot = (expert[:, None] == jnp.arange(E, dtype=jnp.int32)[None, :]).astype(jnp.int32)
    csum = jnp.cumsum(onehot, axis=0)
    counts = csum[-1]
    rank = jnp.sum((csum - 1) * onehot, axis=1)
    padded = (counts + tm - 1) // tm * tm
    ends = jnp.cumsum(padded)
    offsets = ends - padded
    pos = (jnp.sum(onehot * offsets[None, :], axis=1) + rank).astype(jnp.int32)
    src = jnp.zeros((R,), jnp.int32).at[pos].set(jnp.arange(A, dtype=jnp.int32) // TOP_K)
    n_rows = ends[-1:].astype(jnp.int32)
    n_tiles = n_rows // tm
    tile_start = jnp.arange(R // tm, dtype=jnp.int32) * tm
    tile_expert = jnp.sum((tile_start[:, None] >= ends[None, :]).astype(jnp.int32), axis=1)
    tile_expert = jnp.minimum(tile_expert, jnp.max(jnp.where(counts > 0, jnp.arange(E), 0))).astype(jnp.int32)
    xs = _gather_rows(h_packed, src, n_rows, R)
    ys = _moe_experts(xs, tile_expert, n_tiles, wg, wu, wd, tm)
    return _moe_combine(pos, ys, x, route, post_w, gate)


def _token_mixer(h, p, qscale, tables):
    proj = _matmul(h, p["w_in"], BF16)
    qT, k, vT, amax = _prep(proj, p["qk_w"], p["hd"], qscale, tables)
    return proj, qT, k, vT, jnp.max(amax[:, 0]), jnp.max(amax[:, 1])


def kernel(x, c, ctx, c_ctx, w_ada, b_ada, pre_norm_mix, post_norm_mix, pre_norm_ffn, post_norm_ffn, w_in, lambda_q1, lambda_k1, lambda_q2, lambda_k2, da_subln, gm_norm_w, gm_norm_b, gm_w_s, gm_b_s, w_branch_attn, w_branch_gmlp, w_out, ffn_w_gate, ffn_w_up, ffn_w_down, moe_w_router, moe_b_router, moe_w_gate, moe_w_up, moe_w_down):
    B, S, D = x.shape
    assert B == 1, "the kernels process one sequence"
    L = w_ada.shape[0]
    hd = da_subln.shape[1]
    dk = lambda_q1.shape[1]
    assert hd == 2 * dk == LANES
    qk_w = w_branch_attn.shape[1]
    gm_w = gm_norm_w.shape[1]
    u_col, v_col = 3 * qk_w, 3 * qk_w + gm_w
    ga_col, gb_col = 3 * qk_w + 2 * gm_w, 3 * qk_w + 2 * gm_w + D
    assert w_in.shape[2] == gb_col + D
    qscale = dk ** -0.5 * LOG2E

    xl, xc = x[0], ctx[0]
    cc = jnp.zeros((8, D), F32).at[0].set(c[0]).at[1].set(c_ctx)
    mod = _ada(cc, w_ada, b_ada)
    mod_l = mod[:, 0].reshape(L, 6, D)
    mod_c = mod[:, 1].reshape(L, 6, D)
    tables = _rope_tables(S, hd)

    hl = _norm_call(xl, pre=(pre_norm_mix[0], mod_l[0, 0], mod_l[0, 1]))[0]
    hc = _norm_call(xc, pre=(pre_norm_mix[0], mod_c[0, 0], mod_c[0, 1]))[0]

    for l in range(L):
        need_ctx = l < L - 1
        last = l == L - 1
        lambda_init = 0.8 - 0.6 * math.exp(-0.3 * l)
        lam = (jnp.exp(jnp.sum(lambda_q1[l] * lambda_k1[l])) - jnp.exp(jnp.sum(lambda_q2[l] * lambda_k2[l]))
               + lambda_init).reshape(1).astype(F32)
        p = {"w_in": w_in[l].astype(BF16), "qk_w": qk_w, "hd": hd}
        w_pa, w_pb, w_o = w_branch_attn[l].astype(BF16), w_branch_gmlp[l].astype(BF16), w_out[l].astype(BF16)
        out_scale = 1.0 - lambda_init
        is_moe = l % 2 == 1
        i = l // 2

        proj_l, qT_l, k_l, vT_l, qmax_l, kmax_l = _token_mixer(hl, p, qscale, tables)
        proj_c, qT_c, k_c, vT_c, qmax_c, kmax_c = _token_mixer(hc, p, qscale, None)
        k_all = jnp.concatenate([k_l, k_c], axis=0)
        vT_all = jnp.concatenate([vT_l, vT_c], axis=1)
        attn_l = _attention(lam, qT_l, qmax_l, k_all, jnp.maximum(kmax_l, kmax_c), vT_all, da_subln[l], out_scale)
        gm_l = _spatial_gating(proj_l, u_col, v_col, gm_norm_w[l], gm_norm_b[l], gm_w_s[l], gm_b_s[l])
        ol = _matmul(_gated_merge(attn_l, gm_l, proj_l, ga_col, gb_col, w_pa, w_pb), w_o, F32)
        if need_ctx:
            attn_c = _attention(lam, qT_c, qmax_c, k_c, kmax_c, vT_c, da_subln[l], out_scale)
            gm_c = _spatial_gating(proj_c, u_col, v_col, gm_norm_w[l], gm_norm_b[l], gm_w_s[l], gm_b_s[l])
            oc = _matmul(_gated_merge(attn_c, gm_c, proj_c, ga_col, gb_col, w_pa, w_pb), w_o, F32)

        def mixer_update(xr, o, m):
            router = (moe_w_router[i], moe_b_router[i]) if is_moe else None
            return _norm_call(xr, update=(o, post_norm_mix[l], m[l, 2]),
                              pre=(pre_norm_ffn[l], m[l, 3], m[l, 4]), router=router, pack_h=is_moe)

        def next_pre(m):
            return None if last else (pre_norm_mix[l + 1], m[l + 1, 0], m[l + 1, 1])

        if is_moe:
            wg, wu, wd = moe_w_gate[i].astype(BF16), moe_w_up[i].astype(BF16), moe_w_down[i].astype(BF16)
            xl, hl2, route = mixer_update(xl, ol, mod_l)
            xl = _moe_layer(xl, hl2, route, wg, wu, wd, post_norm_ffn[l], mod_l[l, 5])
            if not last:
                hl = _norm_call(xl, pre=next_pre(mod_l))[0]
            if need_ctx:
                xc, hc2, route_c = mixer_update(xc, oc, mod_c)
                xc = _moe_layer(xc, hc2, route_c, wg, wu, wd, post_norm_ffn[l], mod_c[l, 5])
                hc = _norm_call(xc, pre=next_pre(mod_c))[0]
        else:
            FF = ffn_w_gate.shape[2]
            FFp = _round_up(FF, 1024)
            wg = jnp.zeros((D, FFp), BF16).at[:, :FF].set(ffn_w_gate[i].astype(BF16))
            wu = jnp.zeros((D, FFp), BF16).at[:, :FF].set(ffn_w_up[i].astype(BF16))
            wd = jnp.zeros((FFp, D), BF16).at[:FF].set(ffn_w_down[i].astype(BF16))

            def dense(xr, o, m):
                xr, h2 = mixer_update(xr, o, m)
                f = _matmul(_swiglu_up(h2, wg, wu), wd, F32)
                res = _norm_call(xr, update=(f, post_norm_ffn[l], m[l, 5]), pre=next_pre(m))
                return res[0], (res[1] if not last else None)

            xl, hl = dense(xl, ol, mod_l)
            if need_ctx:
                xc, hc = dense(xc, oc, mod_c)
            elif not last:
                hc = _norm_call(xc, pre=next_pre(mod_c))[0]
    return xl[None]
```

```python
import functools
import math

import jax
import jax.numpy as jnp
from jax import lax
from jax.experimental import pallas as pl
from jax.experimental.pallas import tpu as pltpu

F32 = jnp.float32
BF16 = jnp.bfloat16
F8 = jnp.float8_e4m3fn

GRID_W = 64
ROPE_THETA = 10000.0
NORM_EPS = 1e-6
TOP_K = 2
LANES = 128
MXU_COLS = 256
ONES_ROWS = 16
VMEM_LIMIT = 56 * 1024 * 1024
LOG2E = 1.4426950408889634


def _params(n_axes):
    return pltpu.CompilerParams(dimension_semantics=("arbitrary",) * n_axes,
                                vmem_limit_bytes=VMEM_LIMIT)


def _dot(a, b):
    return jnp.dot(a, b, preferred_element_type=F32)


def _round_up(n, m):
    return (n + m - 1) // m * m


def _tile(n, pref):
    if n <= pref:
        return n
    t = pref - pref % LANES
    while n % t:
        t -= LANES
    return t


def _ada_kernel(c_ref, w_ref, b_ref, o_ref):
    c = c_ref[...]
    s = (c * jax.nn.sigmoid(c)).astype(BF16)
    o_ref[0] = _dot(s, w_ref[0].astype(BF16)) + b_ref[0]


def _ada(cc, w_ada, b_ada):
    L, D, N = w_ada.shape
    R = cc.shape[0]
    tn = _tile(N, 512)
    return pl.pallas_call(
        _ada_kernel,
        grid=(L, N // tn),
        in_specs=[pl.BlockSpec((R, D), lambda l, j: (0, 0)),
                  pl.BlockSpec((1, D, tn), lambda l, j: (l, 0, j)),
                  pl.BlockSpec((1, 1, tn), lambda l, j: (l, 0, j))],
        out_specs=pl.BlockSpec((1, R, tn), lambda l, j: (l, 0, j)),
        out_shape=jax.ShapeDtypeStruct((L, R, N), F32),
        compiler_params=_params(2),
        name="ada",
    )(cc, w_ada, b_ada.reshape(L, 1, N))


def _top2_rows(logits, n_experts):
    lane = lax.broadcasted_iota(jnp.int32, logits.shape, 1)
    lane_f = lane.astype(F32)
    neg = jnp.float32(-jnp.inf)
    lg = jnp.where(lane < n_experts, logits, neg)
    v1 = jnp.max(lg, axis=-1, keepdims=True)
    i1 = jnp.min(jnp.where(lg == v1, lane_f, float(LANES)), axis=-1, keepdims=True)
    lg2 = jnp.where(lane_f == i1, neg, lg)
    v2 = jnp.max(lg2, axis=-1, keepdims=True)
    i2 = jnp.min(jnp.where(lg2 == v2, lane_f, float(LANES)), axis=-1, keepdims=True)
    e = jnp.exp(v2 - v1)
    w1 = 1.0 / (1.0 + e)
    w2 = e / (1.0 + e)
    return jnp.where(lane == 0, i1, jnp.where(lane == 1, i2, jnp.where(lane == 2, w1, jnp.where(lane == 3, w2, 0.0))))


def _norm_kernel(*refs, has_update, has_pre, pack_h, n_experts):
    it = iter(refs)
    x_ref = next(it)
    if has_update:
        u_ref, pw_ref, g_ref = next(it), next(it), next(it)
    if has_pre:
        w_ref, sh_ref, sc_ref = next(it), next(it), next(it)
    if n_experts:
        wr_ref, br_ref = next(it), next(it)
    if has_update:
        xo_ref = next(it)
    if has_pre:
        h_ref = next(it)
    if n_experts:
        r_ref = next(it)

    x = x_ref[...]
    if has_update:
        u = u_ref[...].astype(F32)
        un = u * lax.rsqrt(jnp.mean(u * u, axis=-1, keepdims=True) + NORM_EPS)
        x = x + g_ref[...] * (un * pw_ref[...])
        xo_ref[...] = x
    if has_pre:
        y = x * lax.rsqrt(jnp.mean(x * x, axis=-1, keepdims=True) + NORM_EPS)
        h = (y * w_ref[...]) * (1.0 + sc_ref[...]) + sh_ref[...]
        if pack_h:
            half = h.shape[1] // 2
            lo = pltpu.bitcast(h[:, :half].astype(BF16).astype(F32), jnp.uint32)
            hi = pltpu.bitcast(h[:, half:].astype(BF16).astype(F32), jnp.uint32)
            h_ref[...] = (lo >> 16) | (hi & jnp.uint32(0xFFFF0000))
        else:
            h_ref[...] = h.astype(BF16)
        if n_experts:
            wr = wr_ref[...]
            h_hi = h.astype(BF16)
            h_lo = (h - h_hi.astype(F32)).astype(BF16)
            w_hi = wr.astype(BF16)
            w_lo = (wr - w_hi.astype(F32)).astype(BF16)
            logits = _dot(h_hi, w_hi) + (_dot(h_hi, w_lo) + _dot(h_lo, w_hi)) + br_ref[...]
            r_ref[...] = _top2_rows(logits, n_experts)


def _norm_call(x, update=None, pre=None, router=None, pack_h=False):
    T, D = x.shape
    tr = min(256, T)
    row = pl.BlockSpec((tr, D), lambda i: (i, 0))
    vec = pl.BlockSpec((1, D), lambda i: (0, 0))
    args, in_specs, out_shape, out_specs = [x], [row], [], []
    if update is not None:
        upd, pw, g = update
        args += [upd, pw.reshape(1, D), g.reshape(1, D)]
        in_specs += [row, vec, vec]
        out_shape.append(jax.ShapeDtypeStruct((T, D), F32))
        out_specs.append(row)
    if pre is not None:
        w, sh, sc = pre
        args += [w.reshape(1, D), sh.reshape(1, D), sc.reshape(1, D)]
        in_specs += [vec, vec, vec]
        if pack_h:
            out_shape.append(jax.ShapeDtypeStruct((T, D // 2), jnp.uint32))
            out_specs.append(pl.BlockSpec((tr, D // 2), lambda i: (i, 0)))
        else:
            out_shape.append(jax.ShapeDtypeStruct((T, D), BF16))
            out_specs.append(row)
    n_experts = 0
    if router is not None:
        wr, br = router
        n_experts = wr.shape[1]
        wr_p = jnp.zeros((D, LANES), F32).at[:, :n_experts].set(wr)
        br_p = jnp.zeros((1, LANES), F32).at[0, :n_experts].set(br)
        args += [wr_p, br_p]
        in_specs += [pl.BlockSpec((D, LANES), lambda i: (0, 0)), pl.BlockSpec((1, LANES), lambda i: (0, 0))]
        out_shape.append(jax.ShapeDtypeStruct((T, LANES), F32))
        out_specs.append(pl.BlockSpec((tr, LANES), lambda i: (i, 0)))
    kern = functools.partial(_norm_kernel, has_update=update is not None, has_pre=pre is not None,
                             pack_h=pack_h, n_experts=n_experts)
    return pl.pallas_call(kern, grid=(T // tr,), in_specs=in_specs, out_specs=out_specs,
                          out_shape=out_shape, compiler_params=_params(1), name="norm")(*args)


def _mm_kernel(a_ref, b_ref, o_ref):
    o_ref[...] = _dot(a_ref[...], b_ref[...]).astype(o_ref.dtype)


def _mm_acc_kernel(a_ref, b_ref, o_ref, acc_ref, *, nk):
    k = pl.program_id(2)
    p = _dot(a_ref[...], b_ref[...])

    @pl.when(k == 0)
    def _():
        acc_ref[...] = p

    @pl.when(k > 0)
    def _():
        acc_ref[...] += p

    @pl.when(k == nk - 1)
    def _():
        o_ref[...] = acc_ref[...].astype(o_ref.dtype)


def _k_tile(K, limit=4096):
    if K <= limit:
        return K
    tk = limit - limit % 256
    while K % tk:
        tk -= 256
    return tk


def _matmul(a, b, out_dtype, tm=1024, tn=1024):
    M, K = a.shape
    N = b.shape[1]
    tm, tn, tk = _tile(M, tm), _tile(N, tn), _k_tile(K)
    out_shape = jax.ShapeDtypeStruct((M, N), out_dtype)
    if tk == K:
        return pl.pallas_call(
            _mm_kernel, grid=(M // tm, N // tn),
            in_specs=[pl.BlockSpec((tm, K), lambda i, j: (i, 0)), pl.BlockSpec((K, tn), lambda i, j: (0, j))],
            out_specs=pl.BlockSpec((tm, tn), lambda i, j: (i, j)),
            out_shape=out_shape, compiler_params=_params(2), name="matmul")(a, b)
    nk = K // tk
    return pl.pallas_call(
        functools.partial(_mm_acc_kernel, nk=nk), grid=(M // tm, N // tn, nk),
        in_specs=[pl.BlockSpec((tm, tk), lambda i, j, k: (i, k)), pl.BlockSpec((tk, tn), lambda i, j, k: (k, j))],
        out_specs=pl.BlockSpec((tm, tn), lambda i, j, k: (i, j)),
        out_shape=out_shape, scratch_shapes=[pltpu.VMEM((tm, tn), F32)],
        compiler_params=_params(3), name="matmul_acc")(a, b)


def _swiglu_kernel(a_ref, wg_ref, wu_ref, o_ref):
    a = a_ref[...]
    g = _dot(a, wg_ref[...])
    u = _dot(a, wu_ref[...])
    o_ref[...] = (g * jax.nn.sigmoid(g) * u).astype(o_ref.dtype)


def _swiglu_up(a, wg, wu, tm=1024, tn=512):
    M, K = a.shape
    N = wg.shape[1]
    tm, tn = _tile(M, tm), _tile(N, tn)
    wspec = pl.BlockSpec((K, tn), lambda i, j: (0, j))
    return pl.pallas_call(
        _swiglu_kernel, grid=(M // tm, N // tn),
        in_specs=[pl.BlockSpec((tm, K), lambda i, j: (i, 0)), wspec, wspec],
        out_specs=pl.BlockSpec((tm, tn), lambda i, j: (i, j)),
        out_shape=jax.ShapeDtypeStruct((M, N), BF16), compiler_params=_params(2), name="swiglu_up")(a, wg, wu)


def _merge_kernel(a1_ref, a2_ref, w1_ref, w2_ref, g1_ref, g2_ref, o_ref):
    y1 = _dot(a1_ref[...], w1_ref[...])
    y2 = _dot(a2_ref[...], w2_ref[...])
    g1 = jax.nn.sigmoid(g1_ref[...].astype(F32))
    g2 = jax.nn.sigmoid(g2_ref[...].astype(F32))
    o_ref[...] = (g1 * y1 + g2 * y2).astype(o_ref.dtype)


def _gated_merge(attn, gm, proj, ga_col, gb_col, w_pa, w_pb, tm=1024, tn=512):
    M, K1 = attn.shape
    K2 = gm.shape[1]
    N = w_pa.shape[1]
    tm, tn = _tile(M, tm), _tile(N, tn)
    ja, jb = ga_col // tn, gb_col // tn
    return pl.pallas_call(
        _merge_kernel, grid=(M // tm, N // tn),
        in_specs=[pl.BlockSpec((tm, K1), lambda i, j: (i, 0)),
                  pl.BlockSpec((tm, K2), lambda i, j: (i, 0)),
                  pl.BlockSpec((K1, tn), lambda i, j: (0, j)),
                  pl.BlockSpec((K2, tn), lambda i, j: (0, j)),
                  pl.BlockSpec((tm, tn), lambda i, j: (i, ja + j)),
                  pl.BlockSpec((tm, tn), lambda i, j: (i, jb + j))],
        out_specs=pl.BlockSpec((tm, tn), lambda i, j: (i, j)),
        out_shape=jax.ShapeDtypeStruct((M, N), BF16), compiler_params=_params(2), name="gated_merge",
    )(attn, gm, w_pa, w_pb, proj, proj)


def _prep_kernel(*refs, heads, hd, rope, qscale):
    if rope:
        q_ref, k_ref, v_ref, cos_ref, sin_ref, qT_ref, ko_ref, vT_ref, amax_ref = refs
        cos, sin = cos_ref[...], sin_ref[...]
        lane = lax.broadcasted_iota(jnp.int32, cos.shape, 1)
        first = (lane % 32) < 16

        def rot(t):
            partner = jnp.where(first, pltpu.roll(t, hd - 16, 1), pltpu.roll(t, 16, 1))
            return t * cos + partner * sin
    else:
        q_ref, k_ref, v_ref, qT_ref, ko_ref, vT_ref, amax_ref = refs

        def rot(t):
            return t
    hv = hd + ONES_ROWS
    ones = jnp.ones((ONES_ROWS, q_ref.shape[0]), BF16)
    qmax = kmax = None
    for h in range(heads):
        sl = slice(h * hd, (h + 1) * hd)
        q = (rot(q_ref[:, sl].astype(F32)) * qscale).astype(BF16)
        k = rot(k_ref[:, sl].astype(F32)).astype(BF16)
        qT_ref[sl, :] = q.astype(F32).T.astype(BF16)
        ko_ref[:, sl] = k
        vT_ref[h * hv:h * hv + hd, :] = v_ref[:, sl].astype(F32).T.astype(BF16)
        vT_ref[h * hv + hd:(h + 1) * hv, :] = ones
        qa, ka = jnp.abs(q.astype(F32)), jnp.abs(k.astype(F32))
        qmax = qa if qmax is None else jnp.maximum(qmax, qa)
        kmax = ka if kmax is None else jnp.maximum(kmax, ka)
    row = lax.broadcasted_iota(jnp.int32, (8, hd), 0)
    amax_ref[0] = jnp.where(row == 0, jnp.max(qmax, axis=0, keepdims=True),
                            jnp.where(row == 1, jnp.max(kmax, axis=0, keepdims=True), 0.0))


def _prep(proj, width, hd, qscale, tables=None):
    T = proj.shape[0]
    tr = min(256, T)
    heads = width // hd
    col = lambda c: pl.BlockSpec((tr, width), lambda i: (i, c))
    args, in_specs = [proj, proj, proj], [col(0), col(1), col(2)]
    if tables is not None:
        args += list(tables)
        in_specs += [pl.BlockSpec((tr, hd), lambda i: (i, 0))] * 2
    vrows = heads * (hd + ONES_ROWS)
    return pl.pallas_call(
        functools.partial(_prep_kernel, heads=heads, hd=hd, rope=tables is not None, qscale=qscale),
        grid=(T // tr,), in_specs=in_specs,
        out_specs=[pl.BlockSpec((width, tr), lambda i: (0, i)), pl.BlockSpec((tr, width), lambda i: (i, 0)),
                   pl.BlockSpec((vrows, tr), lambda i: (0, i)), pl.BlockSpec((1, 8, hd), lambda i: (i, 0, 0))],
        out_shape=[jax.ShapeDtypeStruct((width, T), BF16), jax.ShapeDtypeStruct((T, width), BF16),
                   jax.ShapeDtypeStruct((vrows, T), BF16), jax.ShapeDtypeStruct((T // tr, 8, hd), F32)],
        compiler_params=_params(1), name="prep")(*args)


def _split8_kernel(sc_ref, k_ref, o_ref, *, heads, hd):
    half = hd // 2
    lane = lax.broadcasted_iota(jnp.int32, (k_ref.shape[0], hd), 1)
    low = lane < half
    for h in range(heads):
        x = k_ref[:, h * hd:(h + 1) * hd].astype(F32) * sc_ref[0]
        hi = x.astype(F8).astype(F32)
        lo = (x - hi).astype(F8).astype(F32)
        hi_r, lo_r = pltpu.roll(hi, half, 1), pltpu.roll(lo, half, 1)
        pieces = (jnp.where(low, hi, lo_r), jnp.where(low, hi, 0.0),
                  jnp.where(low, hi_r, lo), jnp.where(low, hi_r, 0.0))
        for i, piece in enumerate(pieces):
            o_ref[:, (4 * h + i) * hd:(4 * h + i + 1) * hd] = piece.astype(F8)


def _split8(k, scale, hd):
    T, width = k.shape
    tr = min(256, T)
    return pl.pallas_call(
        functools.partial(_split8_kernel, heads=width // hd, hd=hd),
        grid=(T // tr,),
        in_specs=[pl.BlockSpec(memory_space=pltpu.SMEM), pl.BlockSpec((tr, width), lambda i: (i, 0))],
        out_specs=pl.BlockSpec((tr, 4 * width), lambda i: (i, 0)),
        out_shape=jax.ShapeDtypeStruct((T, 4 * width), F8),
        compiler_params=_params(1), name="split8")(scale, k)


def _fp8_scales(qmax, kmax):
    def pow2(m):
        e = jnp.floor(jnp.log2(256.0 / jnp.maximum(m, 1e-30)))
        return jnp.clip(e, -60.0, 60.0)
    eq, ek = pow2(qmax), pow2(kmax)
    return jnp.exp2(eq), jnp.exp2(ek), jnp.exp2(-(eq + ek))


def _rope_tables(n_tokens, hd):
    freqs = hd // 8
    t = jnp.arange(n_tokens)
    inv = 1.0 / (ROPE_THETA ** (jnp.arange(freqs, dtype=F32) / freqs))
    ar = (t // GRID_W).astype(F32)[:, None] * inv
    ac = (t % GRID_W).astype(F32)[:, None] * inv
    cos = jnp.concatenate([jnp.cos(ar), jnp.cos(ar), jnp.cos(ac), jnp.cos(ac)] * 2, axis=1)
    sin = jnp.concatenate([-jnp.sin(ar), jnp.sin(ar), -jnp.sin(ac), jnp.sin(ac)] * 2, axis=1)
    return cos, sin


def _attn_kernel(sc_ref, qT_ref, k_ref, vT_ref, sub_ref, o_ref,
                 q8_ref, s_ref, mt_ref, m_ref, acc_ref, *, n, tq, dk, out_scale):
    j = pl.program_id(2)
    W = 2 * tq
    dv = 2 * dk
    kw = 4 * dk
    n_groups = W // MXU_COLS
    per_map = tq // MXU_COLS

    def scores(slot, c):
        cols = slice(c * MXU_COLS, (c + 1) * MXU_COLS)
        mp, qc = c // per_map, c % per_map
        s = _dot(k_ref[:, mp * kw:(mp + 1) * kw],
                 q8_ref[mp, :, qc * MXU_COLS:(qc + 1) * MXU_COLS]).astype(BF16)
        s_ref[slot, :, cols] = s
        mt_ref[slot, :, cols] = jnp.max(s, axis=0, keepdims=True).astype(F32)

    def consume(slot, c):
        cols = slice(c * MXU_COLS, (c + 1) * MXU_COLS)
        unscale = sc_ref[2]
        m_prev = m_ref[:, cols]
        m_new = jnp.maximum(m_prev, mt_ref[slot, :, cols])
        alpha = jnp.exp2((m_prev - m_new) * unscale)
        p = jnp.exp2((s_ref[slot, :, cols] - m_new.astype(BF16)) * unscale.astype(BF16))
        acc_ref[:, cols] = alpha * acc_ref[:, cols] + _dot(vT_ref[...], p)
        m_ref[:, cols] = m_new

    @pl.when(j == 0)
    def _first():
        q = qT_ref[...].astype(F32) * sc_ref[1]
        hi = q.astype(F8)
        lo = (q - hi.astype(F32)).astype(F8)
        for mp in range(2):
            rows = slice(mp * dk, (mp + 1) * dk)
            q8_ref[mp, 0:dk, :] = hi[rows]
            q8_ref[mp, dk:2 * dk, :] = hi[rows]
            q8_ref[mp, 2 * dk:3 * dk, :] = lo[rows]
            q8_ref[mp, 3 * dk:kw, :] = jnp.zeros((dk, tq), F8)
        m_ref[...] = jnp.full(m_ref.shape, -jnp.inf, F32)
        acc_ref[...] = jnp.zeros(acc_ref.shape, F32)
        for c in range(n_groups):
            scores(0, c)

    for parity in range(2):
        @pl.when((j > 0) & (j < n) & (j % 2 == parity))
        def _steady():
            for c in range(n_groups):
                scores(parity, c)
                consume(1 - parity, c)

    @pl.when(j == n)
    def _last():
        for c in range(n_groups):
            consume((n - 1) % 2, c)
        o_both = acc_ref[0:dv, :] / acc_ref[dv:dv + 1, :]
        o = o_both[:, 0:tq] - sc_ref[0] * o_both[:, tq:W]
        y = o * lax.rsqrt(jnp.mean(o * o, axis=0, keepdims=True) + NORM_EPS)
        y = y * (sub_ref[...] * out_scale)
        o_ref[...] = y.T.astype(o_ref.dtype)


def _attention(lam, qT, qmax, k, kmax, vT, subln, out_scale, tq=2048, tk=1280):
    W, Sq = qT.shape
    hd = subln.shape[0]
    hv = hd + ONES_ROWS
    H = W // hd
    T = k.shape[0]
    tq, tk = _tile(Sq, tq), _tile(T, tk)
    n = T // tk
    q_scale, k_scale, unscale = _fp8_scales(qmax, kmax)
    k8 = _split8(k, k_scale.reshape(1).astype(F32), hd)
    scalars = jnp.stack([lam.reshape(()), q_scale, unscale]).astype(F32)
    return pl.pallas_call(
        functools.partial(_attn_kernel, n=n, tq=tq, dk=hd // 2, out_scale=out_scale),
        grid=(H, Sq // tq, n + 1),
        in_specs=[pl.BlockSpec(memory_space=pltpu.SMEM),
                  pl.BlockSpec((hd, tq), lambda h, i, j: (h, i)),
                  pl.BlockSpec((tk, 4 * hd), lambda h, i, j: (jnp.minimum(j, n - 1), h)),
                  pl.BlockSpec((hv, tk), lambda h, i, j: (h, jnp.maximum(j - 1, 0))),
                  pl.BlockSpec((hd, 1), lambda h, i, j: (0, 0))],
        out_specs=pl.BlockSpec((tq, hd), lambda h, i, j: (i, h)),
        out_shape=jax.ShapeDtypeStruct((Sq, W), BF16),
        scratch_shapes=[pltpu.VMEM((2, 2 * hd, tq), F8), pltpu.VMEM((2, tk, 2 * tq), BF16),
                        pltpu.VMEM((2, 1, 2 * tq), F32), pltpu.VMEM((1, 2 * tq), F32),
                        pltpu.VMEM((hv, 2 * tq), F32)],
        compiler_params=_params(3), name="diff_attention")(scalars, qT, k8, vT, subln.reshape(hd, 1))


def _gmlp_kernel(u_ref, v_ref, gw_ref, gb_ref, ws_ref, bs_ref, o_ref, *, groups, chunk, gdim):
    v = v_ref[...].astype(F32)
    vc = v - jnp.mean(v, axis=-1, keepdims=True)
    vn = vc * lax.rsqrt(jnp.mean(vc * vc, axis=-1, keepdims=True) + NORM_EPS)
    vn = (vn * gw_ref[...] + gb_ref[...]).astype(BF16)
    n_chunks = v.shape[0] // chunk
    for g in range(groups):
        cols = slice(g * gdim, (g + 1) * gdim)
        rhs = jnp.concatenate([vn[c * chunk:(c + 1) * chunk, cols] for c in range(n_chunks)], axis=1)
        mixed = _dot(ws_ref[g], rhs)
        for c in range(n_chunks):
            rows = slice(c * chunk, (c + 1) * chunk)
            m = mixed[:, c * gdim:(c + 1) * gdim] + bs_ref[g]
            o_ref[rows, cols] = (u_ref[rows, cols].astype(F32) * m).astype(o_ref.dtype)


def _spatial_gating(proj, u_col, v_col, gn_w, gn_b, w_s, b_s):
    T = proj.shape[0]
    G, chunk, _ = w_s.shape
    GW = gn_w.shape[0]
    gdim = GW // G
    tr = min(512, T)
    bias = jnp.broadcast_to(b_s[:, :, None], (G, chunk, gdim)).astype(F32)
    vec = pl.BlockSpec((1, GW), lambda i: (0, 0))
    return pl.pallas_call(
        functools.partial(_gmlp_kernel, groups=G, chunk=chunk, gdim=gdim),
        grid=(T // tr,),
        in_specs=[pl.BlockSpec((tr, GW), lambda i: (i, u_col // GW)),
                  pl.BlockSpec((tr, GW), lambda i: (i, v_col // GW)),
                  vec, vec,
                  pl.BlockSpec((G, chunk, chunk), lambda i: (0, 0, 0)),
                  pl.BlockSpec((G, chunk, gdim), lambda i: (0, 0, 0))],
        out_specs=pl.BlockSpec((tr, GW), lambda i: (i, 0)),
        out_shape=jax.ShapeDtypeStruct((T, GW), BF16),
        compiler_params=_params(1), name="spatial_gating",
    )(proj, proj, gn_w.reshape(1, GW), gn_b.reshape(1, GW), w_s.astype(BF16), bias)


def _gather_rows_kernel(src_ref, nrows_ref, h_hbm, o_ref, sem, *, tg):
    base = pl.program_id(0) * tg

    @pl.when(base < nrows_ref[0])
    def _():
        def start(r, carry):
            pltpu.make_async_copy(h_hbm.at[pl.ds(src_ref[base + r], 1)], o_ref.at[pl.ds(r, 1)], sem).start()
            return carry

        def wait(r, carry):
            pltpu.make_async_copy(h_hbm.at[pl.ds(0, 1)], o_ref.at[pl.ds(r, 1)], sem).wait()
            return carry

        lax.fori_loop(0, tg, start, 0)
        lax.fori_loop(0, tg, wait, 0)

    @pl.when(base >= nrows_ref[0])
    def _():
        o_ref[...] = jnp.zeros(o_ref.shape, o_ref.dtype)


def _gather_rows(h, src, n_rows, R, tg=256):
    W = h.shape[1]
    return pl.pallas_call(
        functools.partial(_gather_rows_kernel, tg=tg),
        grid_spec=pltpu.PrefetchScalarGridSpec(
            num_scalar_prefetch=2, grid=(R // tg,),
            in_specs=[pl.BlockSpec(memory_space=pl.ANY)],
            out_specs=pl.BlockSpec((tg, W), lambda i, src, n: (i, 0)),
            scratch_shapes=[pltpu.SemaphoreType.DMA(())]),
        out_shape=jax.ShapeDtypeStruct((R, W), h.dtype),
        compiler_params=_params(1), name="moe_gather")(src, n_rows, h)


def _unpack_bf16_pairs(w):
    lo = pltpu.bitcast(w << 16, F32).astype(BF16)
    hi = pltpu.bitcast(w & jnp.uint32(0xFFFF0000), F32).astype(BF16)
    return jnp.concatenate([lo, hi], axis=1)


def _moe_up_kernel(te_ref, nt_ref, x_ref, wg_ref, wu_ref, o_ref, xb_ref):
    t, j = pl.program_id(0), pl.program_id(1)

    @pl.when(t < nt_ref[0])
    def _():
        @pl.when(j == 0)
        def _():
            xb_ref[...] = _unpack_bf16_pairs(x_ref[...])

        a = xb_ref[...]
        g = _dot(a, wg_ref[0])
        u = _dot(a, wu_ref[0])
        o_ref[...] = (g * jax.nn.sigmoid(g) * u).astype(o_ref.dtype)

    @pl.when(t >= nt_ref[0])
    def _():
        o_ref[...] = jnp.zeros(o_ref.shape, o_ref.dtype)


def _moe_down_kernel(te_ref, nt_ref, a_ref, w_ref, o_ref):
    @pl.when(pl.program_id(0) < nt_ref[0])
    def _():
        o_ref[...] = _dot(a_ref[...], w_ref[0]).astype(o_ref.dtype)

    @pl.when(pl.program_id(0) >= nt_ref[0])
    def _():
        o_ref[...] = jnp.zeros(o_ref.shape, o_ref.dtype)


def _moe_experts(xs, tile_expert, n_tiles, wg, wu, wd, tm, tf=512, tn=1024):
    R = xs.shape[0]
    E, D, F = wg.shape
    NT = R // tm
    tf, tn = _tile(F, tf), _tile(D, tn)
    tile = lambda t, nt: jnp.minimum(t, nt[0] - 1)

    def wmap(last_j):
        return lambda t, j, te, nt: (te[t], 0, jnp.where(t < nt[0], j, last_j))

    hidden = pl.pallas_call(
        _moe_up_kernel,
        grid_spec=pltpu.PrefetchScalarGridSpec(
            num_scalar_prefetch=2, grid=(NT, F // tf),
            in_specs=[pl.BlockSpec((tm, D // 2), lambda t, j, te, nt: (tile(t, nt), 0)),
                      pl.BlockSpec((1, D, tf), wmap(F // tf - 1)),
                      pl.BlockSpec((1, D, tf), wmap(F // tf - 1))],
            out_specs=pl.BlockSpec((tm, tf), lambda t, j, te, nt: (t, j)),
            scratch_shapes=[pltpu.VMEM((tm, D), BF16)]),
        out_shape=jax.ShapeDtypeStruct((R, F), BF16),
        compiler_params=_params(2), name="moe_up")(tile_expert, n_tiles, xs, wg, wu)
    return pl.pallas_call(
        _moe_down_kernel,
        grid_spec=pltpu.PrefetchScalarGridSpec(
            num_scalar_prefetch=2, grid=(NT, D // tn),
            in_specs=[pl.BlockSpec((tm, F), lambda t, j, te, nt: (tile(t, nt), 0)),
                      pl.BlockSpec((1, F, tn), wmap(D // tn - 1))],
            out_specs=pl.BlockSpec((tm, tn), lambda t, j, te, nt: (t, j))),
        out_shape=jax.ShapeDtypeStruct((R, D), F32),
        compiler_params=_params(2), name="moe_down")(tile_expert, n_tiles, hidden, wd)


def _moe_combine_kernel(pos_ref, ys_hbm, x_ref, wt_ref, pw_ref, g_ref, o_ref, buf_ref, sem, *, tr):
    base = pl.program_id(0) * tr * TOP_K

    def start(r, carry):
        for kk in range(TOP_K):
            pltpu.make_async_copy(ys_hbm.at[pl.ds(pos_ref[base + TOP_K * r + kk], 1)],
                                  buf_ref.at[kk, pl.ds(r, 1)], sem).start()
        return carry

    def wait(r, carry):
        for kk in range(TOP_K):
            pltpu.make_async_copy(ys_hbm.at[pl.ds(0, 1)], buf_ref.at[kk, pl.ds(r, 1)], sem).wait()
        return carry

    lax.fori_loop(0, tr, start, 0)
    lax.fori_loop(0, tr, wait, 0)
    wt = wt_ref[...]
    f = wt[:, 2:3] * buf_ref[0] + wt[:, 3:4] * buf_ref[1]
    fn = f * lax.rsqrt(jnp.mean(f * f, axis=-1, keepdims=True) + NORM_EPS)
    o_ref[...] = x_ref[...] + g_ref[...] * (fn * pw_ref[...])


def _moe_combine(pos, ys, x, route, post_w, gate, tr=128):
    T, D = x.shape
    tr = min(tr, T)
    vec = pl.BlockSpec((1, D), lambda i, p: (0, 0))
    return pl.pallas_call(
        functools.partial(_moe_combine_kernel, tr=tr),
        grid_spec=pltpu.PrefetchScalarGridSpec(
            num_scalar_prefetch=1, grid=(T // tr,),
            in_specs=[pl.BlockSpec(memory_space=pl.ANY),
                      pl.BlockSpec((tr, D), lambda i, p: (i, 0)),
                      pl.BlockSpec((tr, LANES), lambda i, p: (i, 0)),
                      vec, vec],
            out_specs=pl.BlockSpec((tr, D), lambda i, p: (i, 0)),
            scratch_shapes=[pltpu.VMEM((TOP_K, tr, D), F32), pltpu.SemaphoreType.DMA(())]),
        out_shape=jax.ShapeDtypeStruct((T, D), F32),
        compiler_params=_params(1), name="moe_combine",
    )(pos, ys, x, route, post_w.reshape(1, D), gate.reshape(1, D))


def _moe_layer(x, h_packed, route, wg, wu, wd, post_w, gate, tm=512):
    T = x.shape[0]
    E = wg.shape[0]
    tm = min(tm, T)
    A = T * TOP_K
    R = _round_up(A, tm) + E * tm
    expert = route[:, :TOP_K].astype(jnp.int32).reshape(A)
    onehot = (expert[:, None] == jnp.arange(E, dtype=jnp.int32)[None, :]).astype(jnp.int32)
    csum = jnp.cumsum(onehot, axis=0)
    counts = csum[-1]
    rank = jnp.sum((csum - 1) * onehot, axis=1)
    padded = (counts + tm - 1) // tm * tm
    ends = jnp.cumsum(padded)
    offsets = ends - padded
    pos = (jnp.sum(onehot * offsets[None, :], axis=1) + rank).astype(jnp.int32)
    src = jnp.zeros((R,), jnp.int32).at[pos].set(jnp.arange(A, dtype=jnp.int32) // TOP_K)
    n_rows = ends[-1:].astype(jnp.int32)
    n_tiles = n_rows // tm
    tile_start = jnp.arange(R // tm, dtype=jnp.int32) * tm
    tile_expert = jnp.sum((tile_start[:, None] >= ends[None, :]).astype(jnp.int32), axis=1)
    tile_expert = jnp.minimum(tile_expert, jnp.max(jnp.where(counts > 0, jnp.arange(E), 0))).astype(jnp.int32)
    xs = _gather_rows(h_packed, src, n_rows, R)
    ys = _moe_experts(xs, tile_expert, n_tiles, wg, wu, wd, tm)
    return _moe_combine(pos, ys, x, route, post_w, gate)


def _token_mixer(h, p, qscale, tables):
    proj = _matmul(h, p["w_in"], BF16)
    qT, k, vT, amax = _prep(proj, p["qk_w"], p["hd"], qscale, tables)
    return proj, qT, k, vT, jnp.max(amax[:, 0]), jnp.max(amax[:, 1])


def kernel(x, c, ctx, c_ctx, w_ada, b_ada, pre_norm_mix, post_norm_mix, pre_norm_ffn, post_norm_ffn, w_in, lambda_q1, lambda_k1, lambda_q2, lambda_k2, da_subln, gm_norm_w, gm_norm_b, gm_w_s, gm_b_s, w_branch_attn, w_branch_gmlp, w_out, ffn_w_gate, ffn_w_up, ffn_w_down, moe_w_router, moe_b_router, moe_w_gate, moe_w_up, moe_w_down):
    B, S, D = x.shape
    assert B == 1, "the kernels process one sequence"
    L = w_ada.shape[0]
    hd = da_subln.shape[1]
    dk = lambda_q1.shape[1]
    assert hd == 2 * dk == LANES
    qk_w = w_branch_attn.shape[1]
    gm_w = gm_norm_w.shape[1]
    u_col, v_col = 3 * qk_w, 3 * qk_w + gm_w
    ga_col, gb_col = 3 * qk_w + 2 * gm_w, 3 * qk_w + 2 * gm_w + D
    assert w_in.shape[2] == gb_col + D
    qscale = dk ** -0.5 * LOG2E

    xl, xc = x[0], ctx[0]
    cc = jnp.zeros((8, D), F32).at[0].set(c[0]).at[1].set(c_ctx)
    mod = _ada(cc, w_ada, b_ada)
    mod_l = mod[:, 0].reshape(L, 6, D)
    mod_c = mod[:, 1].reshape(L, 6, D)
    tables = _rope_tables(S, hd)

    hl = _norm_call(xl, pre=(pre_norm_mix[0], mod_l[0, 0], mod_l[0, 1]))[0]
    hc = _norm_call(xc, pre=(pre_norm_mix[0], mod_c[0, 0], mod_c[0, 1]))[0]

    for l in range(L):
        need_ctx = l < L - 1
        last = l == L - 1
        lambda_init = 0.8 - 0.6 * math.exp(-0.3 * l)
        lam = (jnp.exp(jnp.sum(lambda_q1[l] * lambda_k1[l])) - jnp.exp(jnp.sum(lambda_q2[l] * lambda_k2[l]))
               + lambda_init).reshape(1).astype(F32)
        p = {"w_in": w_in[l].astype(BF16), "qk_w": qk_w, "hd": hd}
        w_pa, w_pb, w_o = w_branch_attn[l].astype(BF16), w_branch_gmlp[l].astype(BF16), w_out[l].astype(BF16)
        out_scale = 1.0 - lambda_init
        is_moe = l % 2 == 1
        i = l // 2

        proj_l, qT_l, k_l, vT_l, qmax_l, kmax_l = _token_mixer(hl, p, qscale, tables)
        proj_c, qT_c, k_c, vT_c, qmax_c, kmax_c = _token_mixer(hc, p, qscale, None)
        k_all = jnp.concatenate([k_l, k_c], axis=0)
        vT_all = jnp.concatenate([vT_l, vT_c], axis=1)
        attn_l = _attention(lam, qT_l, qmax_l, k_all, jnp.maximum(kmax_l, kmax_c), vT_all, da_subln[l], out_scale)
        gm_l = _spatial_gating(proj_l, u_col, v_col, gm_norm_w[l], gm_norm_b[l], gm_w_s[l], gm_b_s[l])
        ol = _matmul(_gated_merge(attn_l, gm_l, proj_l, ga_col, gb_col, w_pa, w_pb), w_o, BF16)
        if need_ctx:
            attn_c = _attention(lam, qT_c, qmax_c, k_c, kmax_c, vT_c, da_subln[l], out_scale)
            gm_c = _spatial_gating(proj_c, u_col, v_col, gm_norm_w[l], gm_norm_b[l], gm_w_s[l], gm_b_s[l])
            oc = _matmul(_gated_merge(attn_c, gm_c, proj_c, ga_col, gb_col, w_pa, w_pb), w_o, BF16)

        def mixer_update(xr, o, m):
            router = (moe_w_router[i], moe_b_router[i]) if is_moe else None
            return _norm_call(xr, update=(o, post_norm_mix[l], m[l, 2]),
                              pre=(pre_norm_ffn[l], m[l, 3], m[l, 4]), router=router, pack_h=is_moe)

        def next_pre(m):
            return None if last else (pre_norm_mix[l + 1], m[l + 1, 0], m[l + 1, 1])

        if is_moe:
            wg, wu, wd = moe_w_gate[i].astype(BF16), moe_w_up[i].astype(BF16), moe_w_down[i].astype(BF16)
            xl, hl2, route = mixer_update(xl, ol, mod_l)
            xl = _moe_layer(xl, hl2, route, wg, wu, wd, post_norm_ffn[l], mod_l[l, 5])
            if not last:
                hl = _norm_call(xl, pre=next_pre(mod_l))[0]
            if need_ctx:
                xc, hc2, route_c = mixer_update(xc, oc, mod_c)
                xc = _moe_layer(xc, hc2, route_c, wg, wu, wd, post_norm_ffn[l], mod_c[l, 5])
                hc = _norm_call(xc, pre=next_pre(mod_c))[0]
        else:
            FF = ffn_w_gate.shape[2]
            FFp = _round_up(FF, 1024)
            wg = jnp.pad(ffn_w_gate[i].astype(BF16), ((0, 0), (0, FFp - FF)))
            wu = jnp.pad(ffn_w_up[i].astype(BF16), ((0, 0), (0, FFp - FF)))
            wd = jnp.pad(ffn_w_down[i].astype(BF16), ((0, FFp - FF), (0, 0)))

            def dense(xr, o, m):
                xr, h2 = mixer_update(xr, o, m)
                f = _matmul(_swiglu_up(h2, wg, wu), wd, BF16)
                res = _norm_call(xr, update=(f, post_norm_ffn[l], m[l, 5]), pre=next_pre(m))
                return res[0], (res[1] if not last else None)

            xl, hl = dense(xl, ol, mod_l)
            if need_ctx:
                xc, hc = dense(xc, oc, mod_c)
            elif not last:
                hc = _norm_call(xc, pre=next_pre(mod_c))[0]
    return xl[None]
```

```python
import functools
import math

import jax
import jax.numpy as jnp
from jax import lax
from jax.experimental import pallas as pl
from jax.experimental.pallas import tpu as pltpu

F32 = jnp.float32
BF16 = jnp.bfloat16
F8 = jnp.float8_e4m3fn

GRID_W = 64
ROPE_THETA = 10000.0
NORM_EPS = 1e-6
TOP_K = 2
LANES = 128
MXU_COLS = 256
ONES_ROWS = 16
VMEM_LIMIT = 56 * 1024 * 1024
LOG2E = 1.4426950408889634


def _params(n_axes):
    return pltpu.CompilerParams(dimension_semantics=("arbitrary",) * n_axes,
                                vmem_limit_bytes=VMEM_LIMIT)


def _dot(a, b):
    return jnp.dot(a, b, preferred_element_type=F32)


def _round_up(n, m):
    return (n + m - 1) // m * m


def _tile(n, pref):
    if n <= pref:
        return n
    t = pref - pref % LANES
    while n % t:
        t -= LANES
    return t


def _ada_kernel(c_ref, w_ref, b_ref, o_ref):
    c = c_ref[...]
    s = (c * jax.nn.sigmoid(c)).astype(BF16)
    o_ref[0] = _dot(s, w_ref[0].astype(BF16)) + b_ref[0]


def _ada(cc, w_ada, b_ada):
    L, D, N = w_ada.shape
    R = cc.shape[0]
    tn = _tile(N, 512)
    return pl.pallas_call(
        _ada_kernel,
        grid=(L, N // tn),
        in_specs=[pl.BlockSpec((R, D), lambda l, j: (0, 0)),
                  pl.BlockSpec((1, D, tn), lambda l, j: (l, 0, j)),
                  pl.BlockSpec((1, 1, tn), lambda l, j: (l, 0, j))],
        out_specs=pl.BlockSpec((1, R, tn), lambda l, j: (l, 0, j)),
        out_shape=jax.ShapeDtypeStruct((L, R, N), F32),
        compiler_params=_params(2),
        name="ada",
    )(cc, w_ada, b_ada.reshape(L, 1, N))


def _top2_rows(logits, n_experts):
    lane = lax.broadcasted_iota(jnp.int32, logits.shape, 1)
    lane_f = lane.astype(F32)
    neg = jnp.float32(-jnp.inf)
    lg = jnp.where(lane < n_experts, logits, neg)
    v1 = jnp.max(lg, axis=-1, keepdims=True)
    i1 = jnp.min(jnp.where(lg == v1, lane_f, float(LANES)), axis=-1, keepdims=True)
    lg2 = jnp.where(lane_f == i1, neg, lg)
    v2 = jnp.max(lg2, axis=-1, keepdims=True)
    i2 = jnp.min(jnp.where(lg2 == v2, lane_f, float(LANES)), axis=-1, keepdims=True)
    e = jnp.exp(v2 - v1)
    w1 = 1.0 / (1.0 + e)
    w2 = e / (1.0 + e)
    return jnp.where(lane == 0, i1, jnp.where(lane == 1, i2, jnp.where(lane == 2, w1, jnp.where(lane == 3, w2, 0.0))))


def _norm_kernel(*refs, has_update, has_pre, pack_h, n_experts):
    it = iter(refs)
    x_ref = next(it)
    if has_update:
        u_ref, pw_ref, g_ref = next(it), next(it), next(it)
    if has_pre:
        w_ref, sh_ref, sc_ref = next(it), next(it), next(it)
    if n_experts:
        wr_ref, br_ref = next(it), next(it)
    if has_update:
        xo_ref = next(it)
    if has_pre:
        h_ref = next(it)
    if n_experts:
        r_ref = next(it)

    x = x_ref[...]
    if has_update:
        u = u_ref[...].astype(F32)
        un = u * lax.rsqrt(jnp.mean(u * u, axis=-1, keepdims=True) + NORM_EPS)
        x = x + g_ref[...] * (un * pw_ref[...])
        xo_ref[...] = x
    if has_pre:
        y = x * lax.rsqrt(jnp.mean(x * x, axis=-1, keepdims=True) + NORM_EPS)
        h = (y * w_ref[...]) * (1.0 + sc_ref[...]) + sh_ref[...]
        if pack_h:
            half = h.shape[1] // 2
            lo = pltpu.bitcast(h[:, :half].astype(BF16).astype(F32), jnp.uint32)
            hi = pltpu.bitcast(h[:, half:].astype(BF16).astype(F32), jnp.uint32)
            h_ref[...] = (lo >> 16) | (hi & jnp.uint32(0xFFFF0000))
        else:
            h_ref[...] = h.astype(BF16)
        if n_experts:
            wr = wr_ref[...]
            h_hi = h.astype(BF16)
            h_lo = (h - h_hi.astype(F32)).astype(BF16)
            w_hi = wr.astype(BF16)
            w_lo = (wr - w_hi.astype(F32)).astype(BF16)
            logits = _dot(h_hi, w_hi) + (_dot(h_hi, w_lo) + _dot(h_lo, w_hi)) + br_ref[...]
            r_ref[...] = _top2_rows(logits, n_experts)


def _norm_call(x, update=None, pre=None, router=None, pack_h=False):
    T, D = x.shape
    tr = min(256, T)
    row = pl.BlockSpec((tr, D), lambda i: (i, 0))
    vec = pl.BlockSpec((1, D), lambda i: (0, 0))
    args, in_specs, out_shape, out_specs = [x], [row], [], []
    if update is not None:
        upd, pw, g = update
        args += [upd, pw.reshape(1, D), g.reshape(1, D)]
        in_specs += [row, vec, vec]
        out_shape.append(jax.ShapeDtypeStruct((T, D), F32))
        out_specs.append(row)
    if pre is not None:
        w, sh, sc = pre
        args += [w.reshape(1, D), sh.reshape(1, D), sc.reshape(1, D)]
        in_specs += [vec, vec, vec]
        if pack_h:
            out_shape.append(jax.ShapeDtypeStruct((T, D // 2), jnp.uint32))
            out_specs.append(pl.BlockSpec((tr, D // 2), lambda i: (i, 0)))
        else:
            out_shape.append(jax.ShapeDtypeStruct((T, D), BF16))
            out_specs.append(row)
    n_experts = 0
    if router is not None:
        wr, br = router
        n_experts = wr.shape[1]
        wr_p = jnp.zeros((D, LANES), F32).at[:, :n_experts].set(wr)
        br_p = jnp.zeros((1, LANES), F32).at[0, :n_experts].set(br)
        args += [wr_p, br_p]
        in_specs += [pl.BlockSpec((D, LANES), lambda i: (0, 0)), pl.BlockSpec((1, LANES), lambda i: (0, 0))]
        out_shape.append(jax.ShapeDtypeStruct((T, LANES), F32))
        out_specs.append(pl.BlockSpec((tr, LANES), lambda i: (i, 0)))
    kern = functools.partial(_norm_kernel, has_update=update is not None, has_pre=pre is not None,
                             pack_h=pack_h, n_experts=n_experts)
    return pl.pallas_call(kern, grid=(T // tr,), in_specs=in_specs, out_specs=out_specs,
                          out_shape=out_shape, compiler_params=_params(1), name="norm")(*args)


def _mm_kernel(a_ref, b_ref, o_ref):
    o_ref[...] = _dot(a_ref[...], b_ref[...]).astype(o_ref.dtype)


def _mm_acc_kernel(a_ref, b_ref, o_ref, acc_ref, *, nk):
    k = pl.program_id(2)
    p = _dot(a_ref[...], b_ref[...])

    @pl.when(k == 0)
    def _():
        acc_ref[...] = p

    @pl.when(k > 0)
    def _():
        acc_ref[...] += p

    @pl.when(k == nk - 1)
    def _():
        o_ref[...] = acc_ref[...].astype(o_ref.dtype)


def _k_tile(K, limit=4096):
    if K <= limit:
        return K
    tk = limit - limit % 256
    while K % tk:
        tk -= 256
    return tk


def _matmul(a, b, out_dtype, tm=1024, tn=1024):
    M, K = a.shape
    N = b.shape[1]
    tm, tn, tk = _tile(M, tm), _tile(N, tn), _k_tile(K)
    out_shape = jax.ShapeDtypeStruct((M, N), out_dtype)
    if tk == K:
        return pl.pallas_call(
            _mm_kernel, grid=(M // tm, N // tn),
            in_specs=[pl.BlockSpec((tm, K), lambda i, j: (i, 0)), pl.BlockSpec((K, tn), lambda i, j: (0, j))],
            out_specs=pl.BlockSpec((tm, tn), lambda i, j: (i, j)),
            out_shape=out_shape, compiler_params=_params(2), name="matmul")(a, b)
    nk = K // tk
    return pl.pallas_call(
        functools.partial(_mm_acc_kernel, nk=nk), grid=(M // tm, N // tn, nk),
        in_specs=[pl.BlockSpec((tm, tk), lambda i, j, k: (i, k)), pl.BlockSpec((tk, tn), lambda i, j, k: (k, j))],
        out_specs=pl.BlockSpec((tm, tn), lambda i, j, k: (i, j)),
        out_shape=out_shape, scratch_shapes=[pltpu.VMEM((tm, tn), F32)],
        compiler_params=_params(3), name="matmul_acc")(a, b)


def _swiglu_kernel(a_ref, wg_ref, wu_ref, o_ref):
    a = a_ref[...]
    g = _dot(a, wg_ref[...])
    u = _dot(a, wu_ref[...])
    o_ref[...] = (g * jax.nn.sigmoid(g) * u).astype(o_ref.dtype)


def _swiglu_up(a, wg, wu, tm=1024, tn=512):
    M, K = a.shape
    N = wg.shape[1]
    tm, tn = _tile(M, tm), _tile(N, tn)
    wspec = pl.BlockSpec((K, tn), lambda i, j: (0, j))
    return pl.pallas_call(
        _swiglu_kernel, grid=(M // tm, N // tn),
        in_specs=[pl.BlockSpec((tm, K), lambda i, j: (i, 0)), wspec, wspec],
        out_specs=pl.BlockSpec((tm, tn), lambda i, j: (i, j)),
        out_shape=jax.ShapeDtypeStruct((M, N), BF16), compiler_params=_params(2), name="swiglu_up")(a, wg, wu)


def _merge_kernel(a1_ref, a2_ref, w1_ref, w2_ref, g1_ref, g2_ref, o_ref):
    y1 = _dot(a1_ref[...], w1_ref[...])
    y2 = _dot(a2_ref[...], w2_ref[...])
    g1 = jax.nn.sigmoid(g1_ref[...].astype(F32))
    g2 = jax.nn.sigmoid(g2_ref[...].astype(F32))
    o_ref[...] = (g1 * y1 + g2 * y2).astype(o_ref.dtype)


def _gated_merge(attn, gm, proj, ga_col, gb_col, w_pa, w_pb, tm=1024, tn=512):
    M, K1 = attn.shape
    K2 = gm.shape[1]
    N = w_pa.shape[1]
    tm, tn = _tile(M, tm), _tile(N, tn)
    ja, jb = ga_col // tn, gb_col // tn
    return pl.pallas_call(
        _merge_kernel, grid=(M // tm, N // tn),
        in_specs=[pl.BlockSpec((tm, K1), lambda i, j: (i, 0)),
                  pl.BlockSpec((tm, K2), lambda i, j: (i, 0)),
                  pl.BlockSpec((K1, tn), lambda i, j: (0, j)),
                  pl.BlockSpec((K2, tn), lambda i, j: (0, j)),
                  pl.BlockSpec((tm, tn), lambda i, j: (i, ja + j)),
                  pl.BlockSpec((tm, tn), lambda i, j: (i, jb + j))],
        out_specs=pl.BlockSpec((tm, tn), lambda i, j: (i, j)),
        out_shape=jax.ShapeDtypeStruct((M, N), BF16), compiler_params=_params(2), name="gated_merge",
    )(attn, gm, w_pa, w_pb, proj, proj)


def _prep_kernel(*refs, heads, hd, rope, qscale):
    if rope:
        q_ref, k_ref, v_ref, cos_ref, sin_ref, qT_ref, ko_ref, vT_ref, amax_ref = refs
        cos, sin = cos_ref[...], sin_ref[...]
        lane = lax.broadcasted_iota(jnp.int32, cos.shape, 1)
        first = (lane % 32) < 16

        def rot(t):
            partner = jnp.where(first, pltpu.roll(t, hd - 16, 1), pltpu.roll(t, 16, 1))
            return t * cos + partner * sin
    else:
        q_ref, k_ref, v_ref, qT_ref, ko_ref, vT_ref, amax_ref = refs

        def rot(t):
            return t
    hv = hd + ONES_ROWS
    ones = jnp.ones((ONES_ROWS, q_ref.shape[0]), BF16)
    qmax = kmax = None
    for h in range(heads):
        sl = slice(h * hd, (h + 1) * hd)
        q = (rot(q_ref[:, sl].astype(F32)) * qscale).astype(BF16)
        k = rot(k_ref[:, sl].astype(F32)).astype(BF16)
        qT_ref[sl, :] = q.astype(F32).T.astype(BF16)
        ko_ref[:, sl] = k
        vT_ref[h * hv:h * hv + hd, :] = v_ref[:, sl].astype(F32).T.astype(BF16)
        vT_ref[h * hv + hd:(h + 1) * hv, :] = ones
        qa, ka = jnp.abs(q.astype(F32)), jnp.abs(k.astype(F32))
        qmax = qa if qmax is None else jnp.maximum(qmax, qa)
        kmax = ka if kmax is None else jnp.maximum(kmax, ka)
    row = lax.broadcasted_iota(jnp.int32, (8, hd), 0)
    blk = jnp.where(row == 0, jnp.max(qmax, axis=0, keepdims=True),
                    jnp.where(row == 1, jnp.max(kmax, axis=0, keepdims=True), 0.0))

    @pl.when(pl.program_id(0) == 0)
    def _():
        amax_ref[...] = blk

    @pl.when(pl.program_id(0) > 0)
    def _():
        amax_ref[...] = jnp.maximum(amax_ref[...], blk)


def _prep(proj, width, hd, qscale, tables=None):
    T = proj.shape[0]
    tr = min(256, T)
    heads = width // hd
    col = lambda c: pl.BlockSpec((tr, width), lambda i: (i, c))
    args, in_specs = [proj, proj, proj], [col(0), col(1), col(2)]
    if tables is not None:
        args += list(tables)
        in_specs += [pl.BlockSpec((tr, hd), lambda i: (i, 0))] * 2
    vrows = heads * (hd + ONES_ROWS)
    return pl.pallas_call(
        functools.partial(_prep_kernel, heads=heads, hd=hd, rope=tables is not None, qscale=qscale),
        grid=(T // tr,), in_specs=in_specs,
        out_specs=[pl.BlockSpec((width, tr), lambda i: (0, i)), pl.BlockSpec((tr, width), lambda i: (i, 0)),
                   pl.BlockSpec((vrows, tr), lambda i: (0, i)), pl.BlockSpec((8, hd), lambda i: (0, 0))],
        out_shape=[jax.ShapeDtypeStruct((width, T), BF16), jax.ShapeDtypeStruct((T, width), BF16),
                   jax.ShapeDtypeStruct((vrows, T), BF16), jax.ShapeDtypeStruct((8, hd), F32)],
        compiler_params=_params(1), name="prep")(*args)


def _split8_kernel(sc_ref, k_ref, o_ref, *, heads, hd):
    half = hd // 2
    lane = lax.broadcasted_iota(jnp.int32, (k_ref.shape[0], hd), 1)
    low = lane < half
    for h in range(heads):
        x = k_ref[:, h * hd:(h + 1) * hd].astype(F32) * sc_ref[0]
        hi = x.astype(F8).astype(F32)
        lo = (x - hi).astype(F8).astype(F32)
        hi_r, lo_r = pltpu.roll(hi, half, 1), pltpu.roll(lo, half, 1)
        pieces = (jnp.where(low, hi, lo_r), jnp.where(low, hi, 0.0),
                  jnp.where(low, hi_r, lo), jnp.where(low, hi_r, 0.0))
        for i, piece in enumerate(pieces):
            o_ref[:, (4 * h + i) * hd:(4 * h + i + 1) * hd] = piece.astype(F8)


def _split8(k, scale, hd):
    T, width = k.shape
    tr = min(256, T)
    return pl.pallas_call(
        functools.partial(_split8_kernel, heads=width // hd, hd=hd),
        grid=(T // tr,),
        in_specs=[pl.BlockSpec(memory_space=pltpu.SMEM), pl.BlockSpec((tr, width), lambda i: (i, 0))],
        out_specs=pl.BlockSpec((tr, 4 * width), lambda i: (i, 0)),
        out_shape=jax.ShapeDtypeStruct((T, 4 * width), F8),
        compiler_params=_params(1), name="split8")(scale, k)


def _fp8_scales(qmax, kmax):
    def pow2(m):
        e = jnp.floor(jnp.log2(256.0 / jnp.maximum(m, 1e-30)))
        return jnp.clip(e, -60.0, 60.0)
    eq, ek = pow2(qmax), pow2(kmax)
    return jnp.exp2(eq), jnp.exp2(ek), jnp.exp2(-(eq + ek))


def _rope_tables(n_tokens, hd):
    freqs = hd // 8
    t = jnp.arange(n_tokens)
    inv = 1.0 / (ROPE_THETA ** (jnp.arange(freqs, dtype=F32) / freqs))
    ar = (t // GRID_W).astype(F32)[:, None] * inv
    ac = (t % GRID_W).astype(F32)[:, None] * inv
    cos = jnp.concatenate([jnp.cos(ar), jnp.cos(ar), jnp.cos(ac), jnp.cos(ac)] * 2, axis=1)
    sin = jnp.concatenate([-jnp.sin(ar), jnp.sin(ar), -jnp.sin(ac), jnp.sin(ac)] * 2, axis=1)
    return cos, sin


def _attn_kernel(sc_ref, qT_ref, k_ref, vT_ref, sub_ref, o_ref,
                 q8_ref, s_ref, mt_ref, m_ref, acc_ref, *, n, tq, dk, out_scale):
    j = pl.program_id(2)
    W = 2 * tq
    dv = 2 * dk
    kw = 4 * dk
    n_groups = W // MXU_COLS
    per_map = tq // MXU_COLS

    def scores(slot, c):
        cols = slice(c * MXU_COLS, (c + 1) * MXU_COLS)
        mp, qc = c // per_map, c % per_map
        s = _dot(k_ref[:, mp * kw:(mp + 1) * kw],
                 q8_ref[mp, :, qc * MXU_COLS:(qc + 1) * MXU_COLS]).astype(BF16)
        s_ref[slot, :, cols] = s
        mt_ref[slot, :, cols] = jnp.max(s, axis=0, keepdims=True).astype(F32)

    def consume(slot, c):
        cols = slice(c * MXU_COLS, (c + 1) * MXU_COLS)
        unscale = sc_ref[2]
        m_prev = m_ref[:, cols]
        m_new = jnp.maximum(m_prev, mt_ref[slot, :, cols])
        alpha = jnp.exp2((m_prev - m_new) * unscale)
        p = jnp.exp2((s_ref[slot, :, cols] - m_new.astype(BF16)) * unscale.astype(BF16))
        acc_ref[:, cols] = alpha * acc_ref[:, cols] + _dot(vT_ref[...], p)
        m_ref[:, cols] = m_new

    @pl.when(j == 0)
    def _first():
        q = qT_ref[...].astype(F32) * sc_ref[1]
        hi = q.astype(F8)
        lo = (q - hi.astype(F32)).astype(F8)
        for mp in range(2):
            rows = slice(mp * dk, (mp + 1) * dk)
            q8_ref[mp, 0:dk, :] = hi[rows]
            q8_ref[mp, dk:2 * dk, :] = hi[rows]
            q8_ref[mp, 2 * dk:3 * dk, :] = lo[rows]
            q8_ref[mp, 3 * dk:kw, :] = jnp.zeros((dk, tq), F8)
        m_ref[...] = jnp.full(m_ref.shape, -jnp.inf, F32)
        acc_ref[...] = jnp.zeros(acc_ref.shape, F32)
        for c in range(n_groups):
            scores(0, c)

    for parity in range(2):
        @pl.when((j > 0) & (j < n) & (j % 2 == parity))
        def _steady():
            for c in range(n_groups):
                scores(parity, c)
                consume(1 - parity, c)

    @pl.when(j == n)
    def _last():
        for c in range(n_groups):
            consume((n - 1) % 2, c)
        o_both = acc_ref[0:dv, :] / acc_ref[dv:dv + 1, :]
        o = o_both[:, 0:tq] - sc_ref[0] * o_both[:, tq:W]
        y = o * lax.rsqrt(jnp.mean(o * o, axis=0, keepdims=True) + NORM_EPS)
        y = y * (sub_ref[...] * out_scale)
        o_ref[...] = y.T.astype(o_ref.dtype)


def _attention(lam, qT, qmax, k, kmax, vT, subln, out_scale, tq=2048, tk=1280):
    W, Sq = qT.shape
    hd = subln.shape[0]
    hv = hd + ONES_ROWS
    H = W // hd
    T = k.shape[0]
    tq, tk = _tile(Sq, tq), _tile(T, tk)
    n = T // tk
    q_scale, k_scale, unscale = _fp8_scales(qmax, kmax)
    k8 = _split8(k, k_scale.reshape(1).astype(F32), hd)
    scalars = jnp.stack([lam.reshape(()), q_scale, unscale]).astype(F32)
    return pl.pallas_call(
        functools.partial(_attn_kernel, n=n, tq=tq, dk=hd // 2, out_scale=out_scale),
        grid=(H, Sq // tq, n + 1),
        in_specs=[pl.BlockSpec(memory_space=pltpu.SMEM),
                  pl.BlockSpec((hd, tq), lambda h, i, j: (h, i)),
                  pl.BlockSpec((tk, 4 * hd), lambda h, i, j: (jnp.minimum(j, n - 1), h)),
                  pl.BlockSpec((hv, tk), lambda h, i, j: (h, jnp.maximum(j - 1, 0))),
                  pl.BlockSpec((hd, 1), lambda h, i, j: (0, 0))],
        out_specs=pl.BlockSpec((tq, hd), lambda h, i, j: (i, h)),
        out_shape=jax.ShapeDtypeStruct((Sq, W), BF16),
        scratch_shapes=[pltpu.VMEM((2, 2 * hd, tq), F8), pltpu.VMEM((2, tk, 2 * tq), BF16),
                        pltpu.VMEM((2, 1, 2 * tq), F32), pltpu.VMEM((1, 2 * tq), F32),
                        pltpu.VMEM((hv, 2 * tq), F32)],
        compiler_params=_params(3), name="diff_attention")(scalars, qT, k8, vT, subln.reshape(hd, 1))


def _gmlp_kernel(u_ref, v_ref, gw_ref, gb_ref, ws_ref, bs_ref, o_ref, *, groups, chunk, gdim):
    v = v_ref[...].astype(F32)
    vc = v - jnp.mean(v, axis=-1, keepdims=True)
    vn = vc * lax.rsqrt(jnp.mean(vc * vc, axis=-1, keepdims=True) + NORM_EPS)
    vn = (vn * gw_ref[...] + gb_ref[...]).astype(BF16)
    n_chunks = v.shape[0] // chunk
    for g in range(groups):
        cols = slice(g * gdim, (g + 1) * gdim)
        rhs = jnp.concatenate([vn[c * chunk:(c + 1) * chunk, cols] for c in range(n_chunks)], axis=1)
        mixed = _dot(ws_ref[g], rhs)
        for c in range(n_chunks):
            rows = slice(c * chunk, (c + 1) * chunk)
            m = mixed[:, c * gdim:(c + 1) * gdim] + bs_ref[g]
            o_ref[rows, cols] = (u_ref[rows, cols].astype(F32) * m).astype(o_ref.dtype)


def _spatial_gating(proj, u_col, v_col, gn_w, gn_b, w_s, b_s):
    T = proj.shape[0]
    G, chunk, _ = w_s.shape
    GW = gn_w.shape[0]
    gdim = GW // G
    tr = min(512, T)
    bias = jnp.broadcast_to(b_s[:, :, None], (G, chunk, gdim)).astype(F32)
    vec = pl.BlockSpec((1, GW), lambda i: (0, 0))
    return pl.pallas_call(
        functools.partial(_gmlp_kernel, groups=G, chunk=chunk, gdim=gdim),
        grid=(T // tr,),
        in_specs=[pl.BlockSpec((tr, GW), lambda i: (i, u_col // GW)),
                  pl.BlockSpec((tr, GW), lambda i: (i, v_col // GW)),
                  vec, vec,
                  pl.BlockSpec((G, chunk, chunk), lambda i: (0, 0, 0)),
                  pl.BlockSpec((G, chunk, gdim), lambda i: (0, 0, 0))],
        out_specs=pl.BlockSpec((tr, GW), lambda i: (i, 0)),
        out_shape=jax.ShapeDtypeStruct((T, GW), BF16),
        compiler_params=_params(1), name="spatial_gating",
    )(proj, proj, gn_w.reshape(1, GW), gn_b.reshape(1, GW), w_s.astype(BF16), bias)


DMA_UNROLL = 8


def _row_copy_loops(n_rows, copies):
    def start_all():
        def body(t, carry):
            for u in range(DMA_UNROLL):
                for c, d in enumerate(copies(t * DMA_UNROLL + u)):
                    d.start(priority=(u + c) % 2)
            return carry
        lax.fori_loop(0, n_rows // DMA_UNROLL, body, 0)

    def wait_all():
        def body(t, carry):
            for u in range(DMA_UNROLL):
                for d in copies(t * DMA_UNROLL + u):
                    d.wait()
            return carry
        lax.fori_loop(0, n_rows // DMA_UNROLL, body, 0)

    return start_all, wait_all


def _gather_rows_kernel(src_ref, nrows_ref, h_hbm, o_ref, buf_ref, sem, *, tg):
    i = pl.program_id(0)
    n_steps = pl.num_programs(0)

    def loops(step, slot):
        def copies(r):
            return [pltpu.make_async_copy(h_hbm.at[pl.ds(src_ref[step * tg + r], 1)],
                                          buf_ref.at[slot, pl.ds(r, 1)], sem.at[slot])]
        return _row_copy_loops(tg, copies)

    def used(step):
        return step * tg < nrows_ref[0]

    @pl.when((i == 0) & used(0))
    def _():
        loops(0, 0)[0]()

    for parity in range(2):
        @pl.when(i % 2 == parity)
        def _():
            @pl.when((i + 1 < n_steps) & used(i + 1))
            def _():
                loops(i + 1, 1 - parity)[0]()

            @pl.when(used(i))
            def _():
                loops(i, parity)[1]()
                o_ref[...] = buf_ref[parity]

            @pl.when(jnp.logical_not(used(i)))
            def _():
                o_ref[...] = jnp.zeros(o_ref.shape, o_ref.dtype)


def _gather_rows(h, src, n_rows, R, tg=256):
    W = h.shape[1]
    assert tg % DMA_UNROLL == 0
    return pl.pallas_call(
        functools.partial(_gather_rows_kernel, tg=tg),
        grid_spec=pltpu.PrefetchScalarGridSpec(
            num_scalar_prefetch=2, grid=(R // tg,),
            in_specs=[pl.BlockSpec(memory_space=pl.ANY)],
            out_specs=pl.BlockSpec((tg, W), lambda i, src, n: (i, 0)),
            scratch_shapes=[pltpu.VMEM((2, tg, W), h.dtype), pltpu.SemaphoreType.DMA((2,))]),
        out_shape=jax.ShapeDtypeStruct((R, W), h.dtype),
        compiler_params=_params(1), name="moe_gather")(src, n_rows, h)


def _unpack_bf16_pairs(w):
    lo = pltpu.bitcast(w << 16, F32).astype(BF16)
    hi = pltpu.bitcast(w & jnp.uint32(0xFFFF0000), F32).astype(BF16)
    return jnp.concatenate([lo, hi], axis=1)


def _moe_up_kernel(te_ref, nt_ref, x_ref, wg_ref, wu_ref, o_ref, xb_ref):
    t, j = pl.program_id(0), pl.program_id(1)

    @pl.when(t < nt_ref[0])
    def _():
        @pl.when(j == 0)
        def _():
            xb_ref[...] = _unpack_bf16_pairs(x_ref[...])

        a = xb_ref[...]
        g = _dot(a, wg_ref[0])
        u = _dot(a, wu_ref[0])
        o_ref[...] = (g * jax.nn.sigmoid(g) * u).astype(o_ref.dtype)

    @pl.when(t >= nt_ref[0])
    def _():
        o_ref[...] = jnp.zeros(o_ref.shape, o_ref.dtype)


def _moe_down_kernel(te_ref, nt_ref, a_ref, w_ref, o_ref):
    @pl.when(pl.program_id(0) < nt_ref[0])
    def _():
        o_ref[...] = _dot(a_ref[...], w_ref[0]).astype(o_ref.dtype)

    @pl.when(pl.program_id(0) >= nt_ref[0])
    def _():
        o_ref[...] = jnp.zeros(o_ref.shape, o_ref.dtype)


def _moe_experts(xs, tile_expert, n_tiles, wg, wu, wd, tm, tf=512, tn=1024):
    R = xs.shape[0]
    E, D, F = wg.shape
    NT = R // tm
    tf, tn = _tile(F, tf), _tile(D, tn)
    tile = lambda t, nt: jnp.minimum(t, nt[0] - 1)

    def wmap(last_j):
        return lambda t, j, te, nt: (te[t], 0, jnp.where(t < nt[0], j, last_j))

    hidden = pl.pallas_call(
        _moe_up_kernel,
        grid_spec=pltpu.PrefetchScalarGridSpec(
            num_scalar_prefetch=2, grid=(NT, F // tf),
            in_specs=[pl.BlockSpec((tm, D // 2), lambda t, j, te, nt: (tile(t, nt), 0)),
                      pl.BlockSpec((1, D, tf), wmap(F // tf - 1)),
                      pl.BlockSpec((1, D, tf), wmap(F // tf - 1))],
            out_specs=pl.BlockSpec((tm, tf), lambda t, j, te, nt: (t, j)),
            scratch_shapes=[pltpu.VMEM((tm, D), BF16)]),
        out_shape=jax.ShapeDtypeStruct((R, F), BF16),
        compiler_params=_params(2), name="moe_up")(tile_expert, n_tiles, xs, wg, wu)
    return pl.pallas_call(
        _moe_down_kernel,
        grid_spec=pltpu.PrefetchScalarGridSpec(
            num_scalar_prefetch=2, grid=(NT, D // tn),
            in_specs=[pl.BlockSpec((tm, F), lambda t, j, te, nt: (tile(t, nt), 0)),
                      pl.BlockSpec((1, F, tn), wmap(D // tn - 1))],
            out_specs=pl.BlockSpec((tm, tn), lambda t, j, te, nt: (t, j))),
        out_shape=jax.ShapeDtypeStruct((R, D), F32),
        compiler_params=_params(2), name="moe_down")(tile_expert, n_tiles, hidden, wd)


def _moe_combine_kernel(pos_ref, ys_hbm, x_ref, wt_ref, pw_ref, g_ref, o_ref, buf_ref, sem, *, tr):
    i = pl.program_id(0)
    n_steps = pl.num_programs(0)

    def loops(step, slot):
        def copies(r):
            return [pltpu.make_async_copy(ys_hbm.at[pl.ds(pos_ref[(step * tr + r) * TOP_K + kk], 1)],
                                          buf_ref.at[slot, kk, pl.ds(r, 1)], sem.at[slot])
                    for kk in range(TOP_K)]
        return _row_copy_loops(tr, copies)

    @pl.when(i == 0)
    def _():
        loops(0, 0)[0]()

    for parity in range(2):
        @pl.when(i % 2 == parity)
        def _():
            @pl.when(i + 1 < n_steps)
            def _():
                loops(i + 1, 1 - parity)[0]()

            loops(i, parity)[1]()
            wt = wt_ref[...]
            f = wt[:, 2:3] * buf_ref[parity, 0] + wt[:, 3:4] * buf_ref[parity, 1]
            fn = f * lax.rsqrt(jnp.mean(f * f, axis=-1, keepdims=True) + NORM_EPS)
            o_ref[...] = x_ref[...] + g_ref[...] * (fn * pw_ref[...])


def _moe_combine(pos, ys, x, route, post_w, gate, tr=128):
    T, D = x.shape
    tr = min(tr, T)
    assert tr % DMA_UNROLL == 0
    vec = pl.BlockSpec((1, D), lambda i, p: (0, 0))
    return pl.pallas_call(
        functools.partial(_moe_combine_kernel, tr=tr),
        grid_spec=pltpu.PrefetchScalarGridSpec(
            num_scalar_prefetch=1, grid=(T // tr,),
            in_specs=[pl.BlockSpec(memory_space=pl.ANY),
                      pl.BlockSpec((tr, D), lambda i, p: (i, 0)),
                      pl.BlockSpec((tr, LANES), lambda i, p: (i, 0)),
                      vec, vec],
            out_specs=pl.BlockSpec((tr, D), lambda i, p: (i, 0)),
            scratch_shapes=[pltpu.VMEM((2, TOP_K, tr, D), F32), pltpu.SemaphoreType.DMA((2,))]),
        out_shape=jax.ShapeDtypeStruct((T, D), F32),
        compiler_params=_params(1), name="moe_combine",
    )(pos, ys, x, route, post_w.reshape(1, D), gate.reshape(1, D))


def _moe_layer(x, h_packed, route, wg, wu, wd, post_w, gate, tm=512):
    T = x.shape[0]
    E = wg.shape[0]
    tm = min(tm, T)
    A = T * TOP_K
    R = _round_up(A, tm) + E * tm
    expert = route[:, :TOP_K].astype(jnp.int32).reshape(A)
    onehot = (expert[:, None] == jnp.arange(E, dtype=jnp.int32)[None, :]).astype(jnp.int32)
    csum = jnp.cumsum(onehot, axis=0)
    counts = csum[-1]
    rank = jnp.sum((csum - 1) * onehot, axis=1)
    padded = (counts + tm - 1) // tm * tm
    ends = jnp.cumsum(padded)
    offsets = ends - padded
    pos = (jnp.sum(onehot * offsets[None, :], axis=1) + rank).astype(jnp.int32)
    src = jnp.zeros((R,), jnp.int32).at[pos].set(jnp.arange(A, dtype=jnp.int32) // TOP_K)
    n_rows = ends[-1:].astype(jnp.int32)
    n_tiles = n_rows // tm
    tile_start = jnp.arange(R // tm, dtype=jnp.int32) * tm
    tile_expert = jnp.sum((tile_start[:, None] >= ends[None, :]).astype(jnp.int32), axis=1)
    tile_expert = jnp.minimum(tile_expert, jnp.max(jnp.where(counts > 0, jnp.arange(E), 0))).astype(jnp.int32)
    xs = _gather_rows(h_packed, src, n_rows, R)
    ys = _moe_experts(xs, tile_expert, n_tiles, wg, wu, wd, tm)
    return _moe_combine(pos, ys, x, route, post_w, gate)


def _token_mixer(h, p, qscale, tables):
    proj = _matmul(h, p["w_in"], BF16)
    qT, k, vT, amax = _prep(proj, p["qk_w"], p["hd"], qscale, tables)
    return proj, qT, k, vT, jnp.max(amax[0]), jnp.max(amax[1])


def kernel(x, c, ctx, c_ctx, w_ada, b_ada, pre_norm_mix, post_norm_mix, pre_norm_ffn, post_norm_ffn, w_in, lambda_q1, lambda_k1, lambda_q2, lambda_k2, da_subln, gm_norm_w, gm_norm_b, gm_w_s, gm_b_s, w_branch_attn, w_branch_gmlp, w_out, ffn_w_gate, ffn_w_up, ffn_w_down, moe_w_router, moe_b_router, moe_w_gate, moe_w_up, moe_w_down):
    B, S, D = x.shape
    assert B == 1, "the kernels process one sequence"
    L = w_ada.shape[0]
    hd = da_subln.shape[1]
    dk = lambda_q1.shape[1]
    assert hd == 2 * dk == LANES
    qk_w = w_branch_attn.shape[1]
    gm_w = gm_norm_w.shape[1]
    u_col, v_col = 3 * qk_w, 3 * qk_w + gm_w
    ga_col, gb_col = 3 * qk_w + 2 * gm_w, 3 * qk_w + 2 * gm_w + D
    assert w_in.shape[2] == gb_col + D
    qscale = dk ** -0.5 * LOG2E

    xl, xc = x[0], ctx[0]
    cc = jnp.zeros((8, D), F32).at[0].set(c[0]).at[1].set(c_ctx)
    mod = _ada(cc, w_ada, b_ada)
    mod_l = mod[:, 0].reshape(L, 6, D)
    mod_c = mod[:, 1].reshape(L, 6, D)
    tables = _rope_tables(S, hd)

    hl = _norm_call(xl, pre=(pre_norm_mix[0], mod_l[0, 0], mod_l[0, 1]))[0]
    hc = _norm_call(xc, pre=(pre_norm_mix[0], mod_c[0, 0], mod_c[0, 1]))[0]

    for l in range(L):
        need_ctx = l < L - 1
        last = l == L - 1
        lambda_init = 0.8 - 0.6 * math.exp(-0.3 * l)
        lam = (jnp.exp(jnp.sum(lambda_q1[l] * lambda_k1[l])) - jnp.exp(jnp.sum(lambda_q2[l] * lambda_k2[l]))
               + lambda_init).reshape(1).astype(F32)
        p = {"w_in": w_in[l].astype(BF16), "qk_w": qk_w, "hd": hd}
        w_pa, w_pb, w_o = w_branch_attn[l].astype(BF16), w_branch_gmlp[l].astype(BF16), w_out[l].astype(BF16)
        out_scale = 1.0 - lambda_init
        is_moe = l % 2 == 1
        i = l // 2

        proj_l, qT_l, k_l, vT_l, qmax_l, kmax_l = _token_mixer(hl, p, qscale, tables)
        proj_c, qT_c, k_c, vT_c, qmax_c, kmax_c = _token_mixer(hc, p, qscale, None)
        k_all = jnp.concatenate([k_l, k_c], axis=0)
        vT_all = jnp.concatenate([vT_l, vT_c], axis=1)
        attn_l = _attention(lam, qT_l, qmax_l, k_all, jnp.maximum(kmax_l, kmax_c), vT_all, da_subln[l], out_scale)
        gm_l = _spatial_gating(proj_l, u_col, v_col, gm_norm_w[l], gm_norm_b[l], gm_w_s[l], gm_b_s[l])
        ol = _matmul(_gated_merge(attn_l, gm_l, proj_l, ga_col, gb_col, w_pa, w_pb), w_o, BF16)
        if need_ctx:
            attn_c = _attention(lam, qT_c, qmax_c, k_c, kmax_c, vT_c, da_subln[l], out_scale)
            gm_c = _spatial_gating(proj_c, u_col, v_col, gm_norm_w[l], gm_norm_b[l], gm_w_s[l], gm_b_s[l])
            oc = _matmul(_gated_merge(attn_c, gm_c, proj_c, ga_col, gb_col, w_pa, w_pb), w_o, BF16)

        def mixer_update(xr, o, m):
            router = (moe_w_router[i], moe_b_router[i]) if is_moe else None
            return _norm_call(xr, update=(o, post_norm_mix[l], m[l, 2]),
                              pre=(pre_norm_ffn[l], m[l, 3], m[l, 4]), router=router, pack_h=is_moe)

        def next_pre(m):
            return None if last else (pre_norm_mix[l + 1], m[l + 1, 0], m[l + 1, 1])

        if is_moe:
            wg, wu, wd = moe_w_gate[i].astype(BF16), moe_w_up[i].astype(BF16), moe_w_down[i].astype(BF16)
            xl, hl2, route = mixer_update(xl, ol, mod_l)
            xl = _moe_layer(xl, hl2, route, wg, wu, wd, post_norm_ffn[l], mod_l[l, 5])
            if not last:
                hl = _norm_call(xl, pre=next_pre(mod_l))[0]
            if need_ctx:
                xc, hc2, route_c = mixer_update(xc, oc, mod_c)
                xc = _moe_layer(xc, hc2, route_c, wg, wu, wd, post_norm_ffn[l], mod_c[l, 5])
                hc = _norm_call(xc, pre=next_pre(mod_c))[0]
        else:
            FF = ffn_w_gate.shape[2]
            FFp = _round_up(FF, 1024)
            wg = jnp.pad(ffn_w_gate[i].astype(BF16), ((0, 0), (0, FFp - FF)))
            wu = jnp.pad(ffn_w_up[i].astype(BF16), ((0, 0), (0, FFp - FF)))
            wd = jnp.pad(ffn_w_down[i].astype(BF16), ((0, FFp - FF), (0, 0)))

            def dense(xr, o, m):
                xr, h2 = mixer_update(xr, o, m)
                f = _matmul(_swiglu_up(h2, wg, wu), wd, BF16)
                res = _norm_call(xr, update=(f, post_norm_ffn[l], m[l, 5]), pre=next_pre(m))
                return res[0], (res[1] if not last else None)

            xl, hl = dense(xl, ol, mod_l)
            if need_ctx:
                xc, hc = dense(xc, oc, mod_c)
            elif not last:
                hc = _norm_call(xc, pre=next_pre(mod_c))[0]
    return xl[None]
```

```python
import functools
import math

import jax
import jax.numpy as jnp
from jax import lax
from jax.experimental import pallas as pl
from jax.experimental.pallas import tpu as pltpu

F32 = jnp.float32
BF16 = jnp.bfloat16
F8 = jnp.float8_e4m3fn

GRID_W = 64
ROPE_THETA = 10000.0
NORM_EPS = 1e-6
TOP_K = 2
LANES = 128
MXU_COLS = 256
ONES_ROWS = 16
VMEM_LIMIT = 56 * 1024 * 1024
LOG2E = 1.4426950408889634


def _params(n_axes):
    return pltpu.CompilerParams(dimension_semantics=("arbitrary",) * n_axes,
                                vmem_limit_bytes=VMEM_LIMIT)


def _dot(a, b):
    return jnp.dot(a, b, preferred_element_type=F32)


def _round_up(n, m):
    return (n + m - 1) // m * m


def _tile(n, pref):
    if n <= pref:
        return n
    t = pref - pref % LANES
    while n % t:
        t -= LANES
    return t


def _ada_kernel(c_ref, w_ref, b_ref, o_ref):
    c = c_ref[...]
    s = (c * jax.nn.sigmoid(c)).astype(BF16)
    o_ref[0] = _dot(s, w_ref[0].astype(BF16)) + b_ref[0]


def _ada(cc, w_ada, b_ada):
    L, D, N = w_ada.shape
    R = cc.shape[0]
    tn = _tile(N, 512)
    return pl.pallas_call(
        _ada_kernel,
        grid=(L, N // tn),
        in_specs=[pl.BlockSpec((R, D), lambda l, j: (0, 0)),
                  pl.BlockSpec((1, D, tn), lambda l, j: (l, 0, j)),
                  pl.BlockSpec((1, 1, tn), lambda l, j: (l, 0, j))],
        out_specs=pl.BlockSpec((1, R, tn), lambda l, j: (l, 0, j)),
        out_shape=jax.ShapeDtypeStruct((L, R, N), F32),
        compiler_params=_params(2),
        name="ada",
    )(cc, w_ada, b_ada.reshape(L, 1, N))


def _top2_rows(logits, n_experts):
    lane = lax.broadcasted_iota(jnp.int32, logits.shape, 1)
    lane_f = lane.astype(F32)
    neg = jnp.float32(-jnp.inf)
    lg = jnp.where(lane < n_experts, logits, neg)
    v1 = jnp.max(lg, axis=-1, keepdims=True)
    i1 = jnp.min(jnp.where(lg == v1, lane_f, float(LANES)), axis=-1, keepdims=True)
    lg2 = jnp.where(lane_f == i1, neg, lg)
    v2 = jnp.max(lg2, axis=-1, keepdims=True)
    i2 = jnp.min(jnp.where(lg2 == v2, lane_f, float(LANES)), axis=-1, keepdims=True)
    e = jnp.exp(v2 - v1)
    w1 = 1.0 / (1.0 + e)
    w2 = e / (1.0 + e)
    return jnp.where(lane == 0, i1, jnp.where(lane == 1, i2, jnp.where(lane == 2, w1, jnp.where(lane == 3, w2, 0.0))))


def _norm_kernel(*refs, has_update, has_pre, pack_h, n_experts):
    it = iter(refs)
    x_ref = next(it)
    if has_update:
        u_ref, pw_ref, g_ref = next(it), next(it), next(it)
    if has_pre:
        w_ref, sh_ref, sc_ref = next(it), next(it), next(it)
    if n_experts:
        wr_ref, br_ref = next(it), next(it)
    if has_update:
        xo_ref = next(it)
    if has_pre:
        h_ref = next(it)
    if n_experts:
        r_ref = next(it)

    x = x_ref[...]
    if has_update:
        u = u_ref[...].astype(F32)
        un = u * lax.rsqrt(jnp.mean(u * u, axis=-1, keepdims=True) + NORM_EPS)
        x = x + g_ref[...] * (un * pw_ref[...])
        xo_ref[...] = x
    if has_pre:
        y = x * lax.rsqrt(jnp.mean(x * x, axis=-1, keepdims=True) + NORM_EPS)
        h = (y * w_ref[...]) * (1.0 + sc_ref[...]) + sh_ref[...]
        if pack_h:
            half = h.shape[1] // 2
            lo = pltpu.bitcast(h[:, :half].astype(BF16).astype(F32), jnp.uint32)
            hi = pltpu.bitcast(h[:, half:].astype(BF16).astype(F32), jnp.uint32)
            h_ref[...] = (lo >> 16) | (hi & jnp.uint32(0xFFFF0000))
        else:
            h_ref[...] = h.astype(BF16)
        if n_experts:
            wr = wr_ref[...]
            h_hi = h.astype(BF16)
            h_lo = (h - h_hi.astype(F32)).astype(BF16)
            w_hi = wr.astype(BF16)
            w_lo = (wr - w_hi.astype(F32)).astype(BF16)
            logits = _dot(h_hi, w_hi) + (_dot(h_hi, w_lo) + _dot(h_lo, w_hi)) + br_ref[...]
            r_ref[...] = _top2_rows(logits, n_experts)


def _norm_call(x, update=None, pre=None, router=None, pack_h=False):
    T, D = x.shape
    tr = min(256, T)
    row = pl.BlockSpec((tr, D), lambda i: (i, 0))
    vec = pl.BlockSpec((1, D), lambda i: (0, 0))
    args, in_specs, out_shape, out_specs = [x], [row], [], []
    if update is not None:
        upd, pw, g = update
        args += [upd, pw.reshape(1, D), g.reshape(1, D)]
        in_specs += [row, vec, vec]
        out_shape.append(jax.ShapeDtypeStruct((T, D), F32))
        out_specs.append(row)
    if pre is not None:
        w, sh, sc = pre
        args += [w.reshape(1, D), sh.reshape(1, D), sc.reshape(1, D)]
        in_specs += [vec, vec, vec]
        if pack_h:
            out_shape.append(jax.ShapeDtypeStruct((T, D // 2), jnp.uint32))
            out_specs.append(pl.BlockSpec((tr, D // 2), lambda i: (i, 0)))
        else:
            out_shape.append(jax.ShapeDtypeStruct((T, D), BF16))
            out_specs.append(row)
    n_experts = 0
    if router is not None:
        wr, br = router
        n_experts = wr.shape[1]
        wr_p = jnp.zeros((D, LANES), F32).at[:, :n_experts].set(wr)
        br_p = jnp.zeros((1, LANES), F32).at[0, :n_experts].set(br)
        args += [wr_p, br_p]
        in_specs += [pl.BlockSpec((D, LANES), lambda i: (0, 0)), pl.BlockSpec((1, LANES), lambda i: (0, 0))]
        out_shape.append(jax.ShapeDtypeStruct((T, LANES), F32))
        out_specs.append(pl.BlockSpec((tr, LANES), lambda i: (i, 0)))
    kern = functools.partial(_norm_kernel, has_update=update is not None, has_pre=pre is not None,
                             pack_h=pack_h, n_experts=n_experts)
    return pl.pallas_call(kern, grid=(T // tr,), in_specs=in_specs, out_specs=out_specs,
                          out_shape=out_shape, compiler_params=_params(1), name="norm")(*args)


def _mm_kernel(a_ref, b_ref, o_ref):
    o_ref[...] = _dot(a_ref[...], b_ref[...]).astype(o_ref.dtype)


def _mm_acc_kernel(a_ref, b_ref, o_ref, acc_ref, *, nk):
    k = pl.program_id(2)
    p = _dot(a_ref[...], b_ref[...])

    @pl.when(k == 0)
    def _():
        acc_ref[...] = p

    @pl.when(k > 0)
    def _():
        acc_ref[...] += p

    @pl.when(k == nk - 1)
    def _():
        o_ref[...] = acc_ref[...].astype(o_ref.dtype)


def _k_tile(K, limit=4096):
    if K <= limit:
        return K
    tk = limit - limit % 256
    while K % tk:
        tk -= 256
    return tk


def _matmul(a, b, out_dtype, tm=1024, tn=1024):
    M, K = a.shape
    N = b.shape[1]
    tm, tn, tk = _tile(M, tm), _tile(N, tn), _k_tile(K)
    out_shape = jax.ShapeDtypeStruct((M, N), out_dtype)
    if tk == K:
        return pl.pallas_call(
            _mm_kernel, grid=(M // tm, N // tn),
            in_specs=[pl.BlockSpec((tm, K), lambda i, j: (i, 0)), pl.BlockSpec((K, tn), lambda i, j: (0, j))],
            out_specs=pl.BlockSpec((tm, tn), lambda i, j: (i, j)),
            out_shape=out_shape, compiler_params=_params(2), name="matmul")(a, b)
    nk = K // tk
    return pl.pallas_call(
        functools.partial(_mm_acc_kernel, nk=nk), grid=(M // tm, N // tn, nk),
        in_specs=[pl.BlockSpec((tm, tk), lambda i, j, k: (i, k)), pl.BlockSpec((tk, tn), lambda i, j, k: (k, j))],
        out_specs=pl.BlockSpec((tm, tn), lambda i, j, k: (i, j)),
        out_shape=out_shape, scratch_shapes=[pltpu.VMEM((tm, tn), F32)],
        compiler_params=_params(3), name="matmul_acc")(a, b)


def _swiglu_kernel(a_ref, wg_ref, wu_ref, o_ref):
    a = a_ref[...]
    g = _dot(a, wg_ref[...])
    u = _dot(a, wu_ref[...])
    o_ref[...] = (g * jax.nn.sigmoid(g) * u).astype(o_ref.dtype)


def _swiglu_up(a, wg, wu, tm=1024, tn=512):
    M, K = a.shape
    N = wg.shape[1]
    tm, tn = _tile(M, tm), _tile(N, tn)
    wspec = pl.BlockSpec((K, tn), lambda i, j: (0, j))
    return pl.pallas_call(
        _swiglu_kernel, grid=(M // tm, N // tn),
        in_specs=[pl.BlockSpec((tm, K), lambda i, j: (i, 0)), wspec, wspec],
        out_specs=pl.BlockSpec((tm, tn), lambda i, j: (i, j)),
        out_shape=jax.ShapeDtypeStruct((M, N), BF16), compiler_params=_params(2), name="swiglu_up")(a, wg, wu)


def _merge_kernel(a1_ref, a2_ref, w1_ref, w2_ref, g1_ref, g2_ref, o_ref):
    y1 = _dot(a1_ref[...], w1_ref[...])
    y2 = _dot(a2_ref[...], w2_ref[...])
    g1 = jax.nn.sigmoid(g1_ref[...].astype(F32))
    g2 = jax.nn.sigmoid(g2_ref[...].astype(F32))
    o_ref[...] = (g1 * y1 + g2 * y2).astype(o_ref.dtype)


def _gated_merge(attn, gm, proj, ga_col, gb_col, w_pa, w_pb, tm=1024, tn=512):
    M, K1 = attn.shape
    K2 = gm.shape[1]
    N = w_pa.shape[1]
    tm, tn = _tile(M, tm), _tile(N, tn)
    ja, jb = ga_col // tn, gb_col // tn
    return pl.pallas_call(
        _merge_kernel, grid=(M // tm, N // tn),
        in_specs=[pl.BlockSpec((tm, K1), lambda i, j: (i, 0)),
                  pl.BlockSpec((tm, K2), lambda i, j: (i, 0)),
                  pl.BlockSpec((K1, tn), lambda i, j: (0, j)),
                  pl.BlockSpec((K2, tn), lambda i, j: (0, j)),
                  pl.BlockSpec((tm, tn), lambda i, j: (i, ja + j)),
                  pl.BlockSpec((tm, tn), lambda i, j: (i, jb + j))],
        out_specs=pl.BlockSpec((tm, tn), lambda i, j: (i, j)),
        out_shape=jax.ShapeDtypeStruct((M, N), BF16), compiler_params=_params(2), name="gated_merge",
    )(attn, gm, w_pa, w_pb, proj, proj)


def _prep_kernel(*refs, heads, hd, rope, qscale):
    if rope:
        q_ref, k_ref, v_ref, cos_ref, sin_ref, qT_ref, ko_ref, vT_ref, amax_ref = refs
        cos, sin = cos_ref[...], sin_ref[...]
        lane = lax.broadcasted_iota(jnp.int32, cos.shape, 1)
        first = (lane % 32) < 16

        def rot(t):
            partner = jnp.where(first, pltpu.roll(t, hd - 16, 1), pltpu.roll(t, 16, 1))
            return t * cos + partner * sin
    else:
        q_ref, k_ref, v_ref, qT_ref, ko_ref, vT_ref, amax_ref = refs

        def rot(t):
            return t
    hv = hd + ONES_ROWS
    ones = jnp.ones((ONES_ROWS, q_ref.shape[0]), BF16)
    qmax = kmax = None
    for h in range(heads):
        sl = slice(h * hd, (h + 1) * hd)
        q = (rot(q_ref[:, sl].astype(F32)) * qscale).astype(BF16)
        k = rot(k_ref[:, sl].astype(F32)).astype(BF16)
        qT_ref[sl, :] = q.astype(F32).T.astype(BF16)
        ko_ref[:, sl] = k
        vT_ref[h * hv:h * hv + hd, :] = v_ref[:, sl].astype(F32).T.astype(BF16)
        vT_ref[h * hv + hd:(h + 1) * hv, :] = ones
        qa, ka = jnp.abs(q.astype(F32)), jnp.abs(k.astype(F32))
        qmax = qa if qmax is None else jnp.maximum(qmax, qa)
        kmax = ka if kmax is None else jnp.maximum(kmax, ka)
    row = lax.broadcasted_iota(jnp.int32, (8, hd), 0)
    blk = jnp.where(row == 0, jnp.max(qmax, axis=0, keepdims=True),
                    jnp.where(row == 1, jnp.max(kmax, axis=0, keepdims=True), 0.0))

    @pl.when(pl.program_id(0) == 0)
    def _():
        amax_ref[...] = blk

    @pl.when(pl.program_id(0) > 0)
    def _():
        amax_ref[...] = jnp.maximum(amax_ref[...], blk)


def _prep(proj, width, hd, qscale, tables=None):
    T = proj.shape[0]
    tr = min(256, T)
    heads = width // hd
    col = lambda c: pl.BlockSpec((tr, width), lambda i: (i, c))
    args, in_specs = [proj, proj, proj], [col(0), col(1), col(2)]
    if tables is not None:
        args += list(tables)
        in_specs += [pl.BlockSpec((tr, hd), lambda i: (i, 0))] * 2
    vrows = heads * (hd + ONES_ROWS)
    return pl.pallas_call(
        functools.partial(_prep_kernel, heads=heads, hd=hd, rope=tables is not None, qscale=qscale),
        grid=(T // tr,), in_specs=in_specs,
        out_specs=[pl.BlockSpec((width, tr), lambda i: (0, i)), pl.BlockSpec((tr, width), lambda i: (i, 0)),
                   pl.BlockSpec((vrows, tr), lambda i: (0, i)), pl.BlockSpec((8, hd), lambda i: (0, 0))],
        out_shape=[jax.ShapeDtypeStruct((width, T), BF16), jax.ShapeDtypeStruct((T, width), BF16),
                   jax.ShapeDtypeStruct((vrows, T), BF16), jax.ShapeDtypeStruct((8, hd), F32)],
        compiler_params=_params(1), name="prep")(*args)


def _split8_kernel(sc_ref, k_ref, o_ref, *, heads, hd):
    half = hd // 2
    lane = lax.broadcasted_iota(jnp.int32, (k_ref.shape[0], hd), 1)
    low = lane < half
    for h in range(heads):
        x = k_ref[:, h * hd:(h + 1) * hd].astype(F32) * sc_ref[0]
        hi = x.astype(F8).astype(F32)
        lo = (x - hi).astype(F8).astype(F32)
        hi_r, lo_r = pltpu.roll(hi, half, 1), pltpu.roll(lo, half, 1)
        pieces = (jnp.where(low, hi, lo_r), jnp.where(low, hi, 0.0),
                  jnp.where(low, hi_r, lo), jnp.where(low, hi_r, 0.0))
        for i, piece in enumerate(pieces):
            o_ref[:, (4 * h + i) * hd:(4 * h + i + 1) * hd] = piece.astype(F8)


def _split8(k, scale, hd):
    T, width = k.shape
    tr = min(256, T)
    return pl.pallas_call(
        functools.partial(_split8_kernel, heads=width // hd, hd=hd),
        grid=(T // tr,),
        in_specs=[pl.BlockSpec(memory_space=pltpu.SMEM), pl.BlockSpec((tr, width), lambda i: (i, 0))],
        out_specs=pl.BlockSpec((tr, 4 * width), lambda i: (i, 0)),
        out_shape=jax.ShapeDtypeStruct((T, 4 * width), F8),
        compiler_params=_params(1), name="split8")(scale, k)


def _fp8_scales(qmax, kmax):
    def pow2(m):
        e = jnp.floor(jnp.log2(256.0 / jnp.maximum(m, 1e-30)))
        return jnp.clip(e, -60.0, 60.0)
    eq, ek = pow2(qmax), pow2(kmax)
    return jnp.exp2(eq), jnp.exp2(ek), jnp.exp2(-(eq + ek))


def _rope_tables(n_tokens, hd):
    freqs = hd // 8
    t = jnp.arange(n_tokens)
    inv = 1.0 / (ROPE_THETA ** (jnp.arange(freqs, dtype=F32) / freqs))
    ar = (t // GRID_W).astype(F32)[:, None] * inv
    ac = (t % GRID_W).astype(F32)[:, None] * inv
    cos = jnp.concatenate([jnp.cos(ar), jnp.cos(ar), jnp.cos(ac), jnp.cos(ac)] * 2, axis=1)
    sin = jnp.concatenate([-jnp.sin(ar), jnp.sin(ar), -jnp.sin(ac), jnp.sin(ac)] * 2, axis=1)
    return cos, sin


SIDE_STEPS = 8


def _attn_kernel(sc_ref, qT_ref, k_ref, vT_ref, sub_ref, *rest, n, tq, dk, out_scale, n_side):
    side_in, o_ref, side_out = rest[:n_side], rest[n_side], rest[n_side + 1:2 * n_side + 1]
    q8_ref, s_ref, mt_ref, m_ref, acc_ref = rest[2 * n_side + 1:]
    _attn_body(sc_ref, qT_ref, k_ref, vT_ref, sub_ref, o_ref, q8_ref, s_ref, mt_ref, m_ref, acc_ref,
               side_in, side_out, n=n, tq=tq, dk=dk, out_scale=out_scale)


def _attn_body(sc_ref, qT_ref, k_ref, vT_ref, sub_ref, o_ref, q8_ref, s_ref, mt_ref, m_ref, acc_ref,
               side_in, side_out, *, n, tq, dk, out_scale):
    j = pl.program_id(2)
    W = 2 * tq
    dv = 2 * dk
    kw = 4 * dk
    n_groups = W // MXU_COLS
    per_map = tq // MXU_COLS

    def scores(slot, c):
        cols = slice(c * MXU_COLS, (c + 1) * MXU_COLS)
        mp, qc = c // per_map, c % per_map
        s = _dot(k_ref[:, mp * kw:(mp + 1) * kw],
                 q8_ref[mp, :, qc * MXU_COLS:(qc + 1) * MXU_COLS]).astype(BF16)
        s_ref[slot, :, cols] = s
        mt_ref[slot, :, cols] = jnp.max(s, axis=0, keepdims=True).astype(F32)

    def consume(slot, c):
        cols = slice(c * MXU_COLS, (c + 1) * MXU_COLS)
        unscale = sc_ref[2]
        m_prev = m_ref[:, cols]
        m_new = jnp.maximum(m_prev, mt_ref[slot, :, cols])
        alpha = jnp.exp2((m_prev - m_new) * unscale)
        p = jnp.exp2((s_ref[slot, :, cols] - m_new.astype(BF16)) * unscale.astype(BF16))
        acc_ref[:, cols] = alpha * acc_ref[:, cols] + _dot(vT_ref[...], p)
        m_ref[:, cols] = m_new

    @pl.when(j == 0)
    def _first():
        q = qT_ref[...].astype(F32) * sc_ref[1]
        hi = q.astype(F8)
        lo = (q - hi.astype(F32)).astype(F8)
        for mp in range(2):
            rows = slice(mp * dk, (mp + 1) * dk)
            q8_ref[mp, 0:dk, :] = hi[rows]
            q8_ref[mp, dk:2 * dk, :] = hi[rows]
            q8_ref[mp, 2 * dk:3 * dk, :] = lo[rows]
            q8_ref[mp, 3 * dk:kw, :] = jnp.zeros((dk, tq), F8)
        m_ref[...] = jnp.full(m_ref.shape, -jnp.inf, F32)
        acc_ref[...] = jnp.zeros(acc_ref.shape, F32)
        for c in range(n_groups):
            scores(0, c)

    for parity in range(2):
        @pl.when((j > 0) & (j < n) & (j % 2 == parity))
        def _steady():
            for c in range(n_groups):
                scores(parity, c)
                consume(1 - parity, c)
            for a_ref, b_ref in zip(side_in, side_out):
                cols = a_ref.shape[1]
                b_ref[:, 0:cols] = a_ref[...].astype(b_ref.dtype)
                if b_ref.shape[1] > cols:
                    b_ref[:, cols:] = jnp.zeros((b_ref.shape[0], b_ref.shape[1] - cols), b_ref.dtype)

    @pl.when(j == n)
    def _last():
        for c in range(n_groups):
            consume((n - 1) % 2, c)
        o_both = acc_ref[0:dv, :] / acc_ref[dv:dv + 1, :]
        o = o_both[:, 0:tq] - sc_ref[0] * o_both[:, tq:W]
        y = o * lax.rsqrt(jnp.mean(o * o, axis=0, keepdims=True) + NORM_EPS)
        y = y * (sub_ref[...] * out_scale)
        o_ref[...] = y.T.astype(o_ref.dtype)


def _attention(lam, qT, qmax, k, kmax, vT, subln, out_scale, side=(), tq=2048, tk=1280):
    W, Sq = qT.shape
    hd = subln.shape[0]
    hv = hd + ONES_ROWS
    H = W // hd
    T = k.shape[0]
    tq, tk = _tile(Sq, tq), _tile(T, tk)
    n = T // tk
    nq = Sq // tq
    q_scale, k_scale, unscale = _fp8_scales(qmax, kmax)
    k8 = _split8(k, k_scale.reshape(1).astype(F32), hd)
    scalars = jnp.stack([lam.reshape(()), q_scale, unscale]).astype(F32)
    plain_casts = []
    if n - 1 < SIDE_STEPS:
        plain_casts = [jnp.pad(w.astype(BF16), ((0, 0), (0, oc - w.shape[1]))) for w, oc in side]
        side = ()
    total = H * nq * SIDE_STEPS

    def chunk_step(h, i, j):
        return (h * nq + i) * SIDE_STEPS + jnp.clip(j - 1, 0, SIDE_STEPS - 1)

    side_in_specs, side_out_specs, side_shapes = [], [], []
    for w, out_cols in side:
        rows, cols = w.shape
        chunks = total
        while chunks > 1 and (total % chunks or rows % chunks or (rows // chunks) % 16):
            chunks -= 1
        assert rows % chunks == 0 and total % chunks == 0
        stride = total // chunks
        imap = lambda h, i, j, stride=stride: (chunk_step(h, i, j) // stride, 0)
        side_in_specs.append(pl.BlockSpec((rows // chunks, cols), imap))
        side_out_specs.append(pl.BlockSpec((rows // chunks, out_cols), imap))
        side_shapes.append(jax.ShapeDtypeStruct((rows, out_cols), BF16))
    outs = pl.pallas_call(
        functools.partial(_attn_kernel, n=n, tq=tq, dk=hd // 2, out_scale=out_scale, n_side=len(side)),
        grid=(H, nq, n + 1),
        in_specs=[pl.BlockSpec(memory_space=pltpu.SMEM),
                  pl.BlockSpec((hd, tq), lambda h, i, j: (h, i)),
                  pl.BlockSpec((tk, 4 * hd), lambda h, i, j: (jnp.minimum(j, n - 1), h)),
                  pl.BlockSpec((hv, tk), lambda h, i, j: (h, jnp.maximum(j - 1, 0))),
                  pl.BlockSpec((hd, 1), lambda h, i, j: (0, 0))] + side_in_specs,
        out_specs=[pl.BlockSpec((tq, hd), lambda h, i, j: (i, h))] + side_out_specs,
        out_shape=[jax.ShapeDtypeStruct((Sq, W), BF16)] + side_shapes,
        scratch_shapes=[pltpu.VMEM((2, 2 * hd, tq), F8), pltpu.VMEM((2, tk, 2 * tq), BF16),
                        pltpu.VMEM((2, 1, 2 * tq), F32), pltpu.VMEM((1, 2 * tq), F32),
                        pltpu.VMEM((hv, 2 * tq), F32)],
        compiler_params=_params(3), name="diff_attention",
    )(scalars, qT, k8, vT, subln.reshape(hd, 1), *[w for w, _ in side])
    return outs[0], list(outs[1:]) + plain_casts


def _gmlp_kernel(u_ref, v_ref, gw_ref, gb_ref, ws_ref, bs_ref, o_ref, *, groups, chunk, gdim):
    v = v_ref[...].astype(F32)
    vc = v - jnp.mean(v, axis=-1, keepdims=True)
    vn = vc * lax.rsqrt(jnp.mean(vc * vc, axis=-1, keepdims=True) + NORM_EPS)
    vn = (vn * gw_ref[...] + gb_ref[...]).astype(BF16)
    n_chunks = v.shape[0] // chunk
    for g in range(groups):
        cols = slice(g * gdim, (g + 1) * gdim)
        rhs = jnp.concatenate([vn[c * chunk:(c + 1) * chunk, cols] for c in range(n_chunks)], axis=1)
        mixed = _dot(ws_ref[g], rhs)
        for c in range(n_chunks):
            rows = slice(c * chunk, (c + 1) * chunk)
            m = mixed[:, c * gdim:(c + 1) * gdim] + bs_ref[g]
            o_ref[rows, cols] = (u_ref[rows, cols].astype(F32) * m).astype(o_ref.dtype)


def _spatial_gating(proj, u_col, v_col, gn_w, gn_b, w_s, b_s):
    T = proj.shape[0]
    G, chunk, _ = w_s.shape
    GW = gn_w.shape[0]
    gdim = GW // G
    tr = min(512, T)
    bias = jnp.broadcast_to(b_s[:, :, None], (G, chunk, gdim)).astype(F32)
    vec = pl.BlockSpec((1, GW), lambda i: (0, 0))
    return pl.pallas_call(
        functools.partial(_gmlp_kernel, groups=G, chunk=chunk, gdim=gdim),
        grid=(T // tr,),
        in_specs=[pl.BlockSpec((tr, GW), lambda i: (i, u_col // GW)),
                  pl.BlockSpec((tr, GW), lambda i: (i, v_col // GW)),
                  vec, vec,
                  pl.BlockSpec((G, chunk, chunk), lambda i: (0, 0, 0)),
                  pl.BlockSpec((G, chunk, gdim), lambda i: (0, 0, 0))],
        out_specs=pl.BlockSpec((tr, GW), lambda i: (i, 0)),
        out_shape=jax.ShapeDtypeStruct((T, GW), BF16),
        compiler_params=_params(1), name="spatial_gating",
    )(proj, proj, gn_w.reshape(1, GW), gn_b.reshape(1, GW), w_s.astype(BF16), bias)


DMA_UNROLL = 8


def _row_copy_loops(n_rows, copies):
    def start_all():
        def body(t, carry):
            for u in range(DMA_UNROLL):
                for c, d in enumerate(copies(t * DMA_UNROLL + u)):
                    d.start(priority=(u + c) % 2)
            return carry
        lax.fori_loop(0, n_rows // DMA_UNROLL, body, 0)

    def wait_all():
        def body(t, carry):
            for u in range(DMA_UNROLL):
                for d in copies(t * DMA_UNROLL + u):
                    d.wait()
            return carry
        lax.fori_loop(0, n_rows // DMA_UNROLL, body, 0)

    return start_all, wait_all


def _gather_rows_kernel(src_ref, nrows_ref, h_hbm, o_ref, buf_ref, sem, *, tg):
    i = pl.program_id(0)
    n_steps = pl.num_programs(0)

    def loops(step, slot):
        def copies(r):
            return [pltpu.make_async_copy(h_hbm.at[pl.ds(src_ref[step * tg + r], 1)],
                                          buf_ref.at[slot, pl.ds(r, 1)], sem.at[slot])]
        return _row_copy_loops(tg, copies)

    def used(step):
        return step * tg < nrows_ref[0]

    @pl.when((i == 0) & used(0))
    def _():
        loops(0, 0)[0]()

    for parity in range(2):
        @pl.when(i % 2 == parity)
        def _():
            @pl.when((i + 1 < n_steps) & used(i + 1))
            def _():
                loops(i + 1, 1 - parity)[0]()

            @pl.when(used(i))
            def _():
                loops(i, parity)[1]()
                o_ref[...] = buf_ref[parity]

            @pl.when(jnp.logical_not(used(i)))
            def _():
                o_ref[...] = jnp.zeros(o_ref.shape, o_ref.dtype)


def _gather_rows(h, src, n_rows, R, tg=256):
    W = h.shape[1]
    assert tg % DMA_UNROLL == 0
    return pl.pallas_call(
        functools.partial(_gather_rows_kernel, tg=tg),
        grid_spec=pltpu.PrefetchScalarGridSpec(
            num_scalar_prefetch=2, grid=(R // tg,),
            in_specs=[pl.BlockSpec(memory_space=pl.ANY)],
            out_specs=pl.BlockSpec((tg, W), lambda i, src, n: (i, 0)),
            scratch_shapes=[pltpu.VMEM((2, tg, W), h.dtype), pltpu.SemaphoreType.DMA((2,))]),
        out_shape=jax.ShapeDtypeStruct((R, W), h.dtype),
        compiler_params=_params(1), name="moe_gather")(src, n_rows, h)


def _unpack_bf16_pairs(w):
    lo = pltpu.bitcast(w << 16, F32).astype(BF16)
    hi = pltpu.bitcast(w & jnp.uint32(0xFFFF0000), F32).astype(BF16)
    return jnp.concatenate([lo, hi], axis=1)


def _moe_up_kernel(te_ref, nt_ref, x_ref, wg_ref, wu_ref, o_ref, xb_ref):
    t, j = pl.program_id(0), pl.program_id(1)

    @pl.when(t < nt_ref[0])
    def _():
        @pl.when(j == 0)
        def _():
            xb_ref[...] = _unpack_bf16_pairs(x_ref[...])

        a = xb_ref[...]
        g = _dot(a, wg_ref[0])
        u = _dot(a, wu_ref[0])
        o_ref[...] = (g * jax.nn.sigmoid(g) * u).astype(o_ref.dtype)

    @pl.when(t >= nt_ref[0])
    def _():
        o_ref[...] = jnp.zeros(o_ref.shape, o_ref.dtype)


def _moe_down_kernel(te_ref, nt_ref, a_ref, w_ref, o_ref):
    @pl.when(pl.program_id(0) < nt_ref[0])
    def _():
        o_ref[...] = _dot(a_ref[...], w_ref[0]).astype(o_ref.dtype)

    @pl.when(pl.program_id(0) >= nt_ref[0])
    def _():
        o_ref[...] = jnp.zeros(o_ref.shape, o_ref.dtype)


def _moe_experts(xs, tile_expert, n_tiles, wg, wu, wd, tm, tf=512, tn=1024):
    R = xs.shape[0]
    E, D, F = wg.shape
    NT = R // tm
    tf, tn = _tile(F, tf), _tile(D, tn)
    tile = lambda t, nt: jnp.minimum(t, nt[0] - 1)

    def wmap(last_j):
        return lambda t, j, te, nt: (te[t], 0, jnp.where(t < nt[0], j, last_j))

    hidden = pl.pallas_call(
        _moe_up_kernel,
        grid_spec=pltpu.PrefetchScalarGridSpec(
            num_scalar_prefetch=2, grid=(NT, F // tf),
            in_specs=[pl.BlockSpec((tm, D // 2), lambda t, j, te, nt: (tile(t, nt), 0)),
                      pl.BlockSpec((1, D, tf), wmap(F // tf - 1)),
                      pl.BlockSpec((1, D, tf), wmap(F // tf - 1))],
            out_specs=pl.BlockSpec((tm, tf), lambda t, j, te, nt: (t, j)),
            scratch_shapes=[pltpu.VMEM((tm, D), BF16)]),
        out_shape=jax.ShapeDtypeStruct((R, F), BF16),
        compiler_params=_params(2), name="moe_up")(tile_expert, n_tiles, xs, wg, wu)
    return pl.pallas_call(
        _moe_down_kernel,
        grid_spec=pltpu.PrefetchScalarGridSpec(
            num_scalar_prefetch=2, grid=(NT, D // tn),
            in_specs=[pl.BlockSpec((tm, F), lambda t, j, te, nt: (tile(t, nt), 0)),
                      pl.BlockSpec((1, F, tn), wmap(D // tn - 1))],
            out_specs=pl.BlockSpec((tm, tn), lambda t, j, te, nt: (t, j))),
        out_shape=jax.ShapeDtypeStruct((R, D), F32),
        compiler_params=_params(2), name="moe_down")(tile_expert, n_tiles, hidden, wd)


def _moe_combine_kernel(pos_ref, ys_hbm, x_ref, wt_ref, pw_ref, g_ref, o_ref, buf_ref, sem, *, tr):
    i = pl.program_id(0)
    n_steps = pl.num_programs(0)

    def loops(step, slot):
        def copies(r):
            return [pltpu.make_async_copy(ys_hbm.at[pl.ds(pos_ref[(step * tr + r) * TOP_K + kk], 1)],
                                          buf_ref.at[slot, kk, pl.ds(r, 1)], sem.at[slot])
                    for kk in range(TOP_K)]
        return _row_copy_loops(tr, copies)

    @pl.when(i == 0)
    def _():
        loops(0, 0)[0]()

    for parity in range(2):
        @pl.when(i % 2 == parity)
        def _():
            @pl.when(i + 1 < n_steps)
            def _():
                loops(i + 1, 1 - parity)[0]()

            loops(i, parity)[1]()
            wt = wt_ref[...]
            f = wt[:, 2:3] * buf_ref[parity, 0] + wt[:, 3:4] * buf_ref[parity, 1]
            fn = f * lax.rsqrt(jnp.mean(f * f, axis=-1, keepdims=True) + NORM_EPS)
            o_ref[...] = x_ref[...] + g_ref[...] * (fn * pw_ref[...])


def _moe_combine(pos, ys, x, route, post_w, gate, tr=128):
    T, D = x.shape
    tr = min(tr, T)
    assert tr % DMA_UNROLL == 0
    vec = pl.BlockSpec((1, D), lambda i, p: (0, 0))
    return pl.pallas_call(
        functools.partial(_moe_combine_kernel, tr=tr),
        grid_spec=pltpu.PrefetchScalarGridSpec(
            num_scalar_prefetch=1, grid=(T // tr,),
            in_specs=[pl.BlockSpec(memory_space=pl.ANY),
                      pl.BlockSpec((tr, D), lambda i, p: (i, 0)),
                      pl.BlockSpec((tr, LANES), lambda i, p: (i, 0)),
                      vec, vec],
            out_specs=pl.BlockSpec((tr, D), lambda i, p: (i, 0)),
            scratch_shapes=[pltpu.VMEM((2, TOP_K, tr, D), F32), pltpu.SemaphoreType.DMA((2,))]),
        out_shape=jax.ShapeDtypeStruct((T, D), F32),
        compiler_params=_params(1), name="moe_combine",
    )(pos, ys, x, route, post_w.reshape(1, D), gate.reshape(1, D))


def _moe_layer(x, h_packed, route, wg, wu, wd, post_w, gate, tm=512):
    T = x.shape[0]
    E = wg.shape[0]
    tm = min(tm, T)
    A = T * TOP_K
    R = _round_up(A, tm) + E * tm
    expert = route[:, :TOP_K].astype(jnp.int32).reshape(A)
    onehot = (expert[:, None] == jnp.arange(E, dtype=jnp.int32)[None, :]).astype(jnp.int32)
    csum = jnp.cumsum(onehot, axis=0)
    counts = csum[-1]
    rank = jnp.sum((csum - 1) * onehot, axis=1)
    padded = (counts + tm - 1) // tm * tm
    ends = jnp.cumsum(padded)
    offsets = ends - padded
    pos = (jnp.sum(onehot * offsets[None, :], axis=1) + rank).astype(jnp.int32)
    src = jnp.zeros((R,), jnp.int32).at[pos].set(jnp.arange(A, dtype=jnp.int32) // TOP_K)
    n_rows = ends[-1:].astype(jnp.int32)
    n_tiles = n_rows // tm
    tile_start = jnp.arange(R // tm, dtype=jnp.int32) * tm
    tile_expert = jnp.sum((tile_start[:, None] >= ends[None, :]).astype(jnp.int32), axis=1)
    tile_expert = jnp.minimum(tile_expert, jnp.max(jnp.where(counts > 0, jnp.arange(E), 0))).astype(jnp.int32)
    xs = _gather_rows(h_packed, src, n_rows, R)
    ys = _moe_experts(xs, tile_expert, n_tiles, wg, wu, wd, tm)
    return _moe_combine(pos, ys, x, route, post_w, gate)


def _token_mixer(h, p, qscale, tables):
    proj = _matmul(h, p["w_in"], BF16)
    qT, k, vT, amax = _prep(proj, p["qk_w"], p["hd"], qscale, tables)
    return proj, qT, k, vT, jnp.max(amax[0]), jnp.max(amax[1])


def kernel(x, c, ctx, c_ctx, w_ada, b_ada, pre_norm_mix, post_norm_mix, pre_norm_ffn, post_norm_ffn, w_in, lambda_q1, lambda_k1, lambda_q2, lambda_k2, da_subln, gm_norm_w, gm_norm_b, gm_w_s, gm_b_s, w_branch_attn, w_branch_gmlp, w_out, ffn_w_gate, ffn_w_up, ffn_w_down, moe_w_router, moe_b_router, moe_w_gate, moe_w_up, moe_w_down):
    B, S, D = x.shape
    assert B == 1, "the kernels process one sequence"
    L = w_ada.shape[0]
    hd = da_subln.shape[1]
    dk = lambda_q1.shape[1]
    assert hd == 2 * dk == LANES
    qk_w = w_branch_attn.shape[1]
    gm_w = gm_norm_w.shape[1]
    u_col, v_col = 3 * qk_w, 3 * qk_w + gm_w
    ga_col, gb_col = 3 * qk_w + 2 * gm_w, 3 * qk_w + 2 * gm_w + D
    assert w_in.shape[2] == gb_col + D
    qscale = dk ** -0.5 * LOG2E

    xl, xc = x[0], ctx[0]
    cc = jnp.zeros((8, D), F32).at[0].set(c[0]).at[1].set(c_ctx)
    mod = _ada(cc, w_ada, b_ada)
    mod_l = mod[:, 0].reshape(L, 6, D)
    mod_c = mod[:, 1].reshape(L, 6, D)
    tables = _rope_tables(S, hd)

    hl = _norm_call(xl, pre=(pre_norm_mix[0], mod_l[0, 0], mod_l[0, 1]))[0]
    hc = _norm_call(xc, pre=(pre_norm_mix[0], mod_c[0, 0], mod_c[0, 1]))[0]

    w_in_next = None
    for l in range(L):
        need_ctx = l < L - 1
        last = l == L - 1
        lambda_init = 0.8 - 0.6 * math.exp(-0.3 * l)
        lam = (jnp.exp(jnp.sum(lambda_q1[l] * lambda_k1[l])) - jnp.exp(jnp.sum(lambda_q2[l] * lambda_k2[l]))
               + lambda_init).reshape(1).astype(F32)
        p = {"w_in": w_in[l].astype(BF16) if w_in_next is None else w_in_next, "qk_w": qk_w, "hd": hd}
        w_pa, w_pb, w_o = w_branch_attn[l].astype(BF16), w_branch_gmlp[l].astype(BF16), w_out[l].astype(BF16)
        out_scale = 1.0 - lambda_init
        is_moe = l % 2 == 1
        i = l // 2

        proj_l, qT_l, k_l, vT_l, qmax_l, kmax_l = _token_mixer(hl, p, qscale, tables)
        proj_c, qT_c, k_c, vT_c, qmax_c, kmax_c = _token_mixer(hc, p, qscale, None)
        k_all = jnp.concatenate([k_l, k_c], axis=0)
        vT_all = jnp.concatenate([vT_l, vT_c], axis=1)
        if is_moe:
            E, _, F = moe_w_gate.shape[1:]
            side = [(moe_w_gate[i].reshape(E * D, F), F), (moe_w_up[i].reshape(E * D, F), F),
                    (moe_w_down[i].reshape(E * F, D), D)]
        else:
            FF = ffn_w_gate.shape[2]
            FFp = _round_up(FF, 1024)
            side = [(ffn_w_gate[i], FFp), (ffn_w_up[i], FFp)]
        if not last:
            side.append((w_in[l + 1], w_in.shape[2]))
        attn_l, casts = _attention(lam, qT_l, qmax_l, k_all, jnp.maximum(kmax_l, kmax_c), vT_all,
                                   da_subln[l], out_scale, side=side)
        w_in_next = None if last else casts[-1]
        gm_l = _spatial_gating(proj_l, u_col, v_col, gm_norm_w[l], gm_norm_b[l], gm_w_s[l], gm_b_s[l])
        ol = _matmul(_gated_merge(attn_l, gm_l, proj_l, ga_col, gb_col, w_pa, w_pb), w_o, BF16)
        if need_ctx:
            attn_c, _ = _attention(lam, qT_c, qmax_c, k_c, kmax_c, vT_c, da_subln[l], out_scale)
            gm_c = _spatial_gating(proj_c, u_col, v_col, gm_norm_w[l], gm_norm_b[l], gm_w_s[l], gm_b_s[l])
            oc = _matmul(_gated_merge(attn_c, gm_c, proj_c, ga_col, gb_col, w_pa, w_pb), w_o, BF16)

        def mixer_update(xr, o, m):
            router = (moe_w_router[i], moe_b_router[i]) if is_moe else None
            return _norm_call(xr, update=(o, post_norm_mix[l], m[l, 2]),
                              pre=(pre_norm_ffn[l], m[l, 3], m[l, 4]), router=router, pack_h=is_moe)

        def next_pre(m):
            return None if last else (pre_norm_mix[l + 1], m[l + 1, 0], m[l + 1, 1])

        if is_moe:
            wg, wu, wd = casts[0].reshape(E, D, F), casts[1].reshape(E, D, F), casts[2].reshape(E, F, D)
            xl, hl2, route = mixer_update(xl, ol, mod_l)
            xl = _moe_layer(xl, hl2, route, wg, wu, wd, post_norm_ffn[l], mod_l[l, 5])
            if not last:
                hl = _norm_call(xl, pre=next_pre(mod_l))[0]
            if need_ctx:
                xc, hc2, route_c = mixer_update(xc, oc, mod_c)
                xc = _moe_layer(xc, hc2, route_c, wg, wu, wd, post_norm_ffn[l], mod_c[l, 5])
                hc = _norm_call(xc, pre=next_pre(mod_c))[0]
        else:
            wg, wu = casts[0], casts[1]
            wd = jnp.pad(ffn_w_down[i].astype(BF16), ((0, FFp - FF), (0, 0)))

            def dense(xr, o, m):
                xr, h2 = mixer_update(xr, o, m)
                f = _matmul(_swiglu_up(h2, wg, wu), wd, BF16)
                res = _norm_call(xr, update=(f, post_norm_ffn[l], m[l, 5]), pre=next_pre(m))
                return res[0], (res[1] if not last else None)

            xl, hl = dense(xl, ol, mod_l)
            if need_ctx:
                xc, hc = dense(xc, oc, mod_c)
            elif not last:
                hc = _norm_call(xc, pre=next_pre(mod_c))[0]
    return xl[None]
```

```python
import functools
import math

import jax
import jax.numpy as jnp
from jax import lax
from jax.experimental import pallas as pl
from jax.experimental.pallas import tpu as pltpu

F32 = jnp.float32
BF16 = jnp.bfloat16
F8 = jnp.float8_e4m3fn

GRID_W = 64
ROPE_THETA = 10000.0
NORM_EPS = 1e-6
TOP_K = 2
LANES = 128
MXU_COLS = 256
ONES_ROWS = 16
VMEM_LIMIT = 56 * 1024 * 1024
LOG2E = 1.4426950408889634


def _params(n_axes):
    return pltpu.CompilerParams(dimension_semantics=("arbitrary",) * n_axes,
                                vmem_limit_bytes=VMEM_LIMIT)


def _dot(a, b):
    return jnp.dot(a, b, preferred_element_type=F32)


def _round_up(n, m):
    return (n + m - 1) // m * m


def _tile(n, pref):
    if n <= pref:
        return n
    t = pref - pref % LANES
    while n % t:
        t -= LANES
    return t


def _ada_kernel(c_ref, w_ref, b_ref, o_ref):
    c = c_ref[...]
    s = (c * jax.nn.sigmoid(c)).astype(BF16)
    o_ref[0] = _dot(s, w_ref[0].astype(BF16)) + b_ref[0]


def _ada(cc, w_ada, b_ada):
    L, D, N = w_ada.shape
    R = cc.shape[0]
    tn = _tile(N, 512)
    return pl.pallas_call(
        _ada_kernel,
        grid=(L, N // tn),
        in_specs=[pl.BlockSpec((R, D), lambda l, j: (0, 0)),
                  pl.BlockSpec((1, D, tn), lambda l, j: (l, 0, j)),
                  pl.BlockSpec((1, 1, tn), lambda l, j: (l, 0, j))],
        out_specs=pl.BlockSpec((1, R, tn), lambda l, j: (l, 0, j)),
        out_shape=jax.ShapeDtypeStruct((L, R, N), F32),
        compiler_params=_params(2),
        name="ada",
    )(cc, w_ada, b_ada.reshape(L, 1, N))


def _top2_rows(logits, n_experts):
    lane = lax.broadcasted_iota(jnp.int32, logits.shape, 1)
    lane_f = lane.astype(F32)
    neg = jnp.float32(-jnp.inf)
    lg = jnp.where(lane < n_experts, logits, neg)
    v1 = jnp.max(lg, axis=-1, keepdims=True)
    i1 = jnp.min(jnp.where(lg == v1, lane_f, float(LANES)), axis=-1, keepdims=True)
    lg2 = jnp.where(lane_f == i1, neg, lg)
    v2 = jnp.max(lg2, axis=-1, keepdims=True)
    i2 = jnp.min(jnp.where(lg2 == v2, lane_f, float(LANES)), axis=-1, keepdims=True)
    e = jnp.exp(v2 - v1)
    w1 = 1.0 / (1.0 + e)
    w2 = e / (1.0 + e)
    return jnp.where(lane == 0, i1, jnp.where(lane == 1, i2, jnp.where(lane == 2, w1, jnp.where(lane == 3, w2, 0.0))))


def _norm_kernel(*refs, has_update, has_pre, pack_h, n_experts):
    it = iter(refs)
    x_ref = next(it)
    if has_update:
        u_ref, pw_ref, g_ref = next(it), next(it), next(it)
    if has_pre:
        w_ref, sh_ref, sc_ref = next(it), next(it), next(it)
    if n_experts:
        wr_ref, br_ref = next(it), next(it)
    if has_update:
        xo_ref = next(it)
    if has_pre:
        h_ref = next(it)
    if n_experts:
        r_ref = next(it)

    x = x_ref[...]
    if has_update:
        u = u_ref[...].astype(F32)
        un = u * lax.rsqrt(jnp.mean(u * u, axis=-1, keepdims=True) + NORM_EPS)
        x = x + g_ref[...] * (un * pw_ref[...])
        xo_ref[...] = x
    if has_pre:
        y = x * lax.rsqrt(jnp.mean(x * x, axis=-1, keepdims=True) + NORM_EPS)
        h = (y * w_ref[...]) * (1.0 + sc_ref[...]) + sh_ref[...]
        if pack_h:
            half = h.shape[1] // 2
            lo = pltpu.bitcast(h[:, :half].astype(BF16).astype(F32), jnp.uint32)
            hi = pltpu.bitcast(h[:, half:].astype(BF16).astype(F32), jnp.uint32)
            h_ref[...] = (lo >> 16) | (hi & jnp.uint32(0xFFFF0000))
        else:
            h_ref[...] = h.astype(BF16)
        if n_experts:
            wr = wr_ref[...]
            h_hi = h.astype(BF16)
            h_lo = (h - h_hi.astype(F32)).astype(BF16)
            w_hi = wr.astype(BF16)
            w_lo = (wr - w_hi.astype(F32)).astype(BF16)
            logits = _dot(h_hi, w_hi) + (_dot(h_hi, w_lo) + _dot(h_lo, w_hi)) + br_ref[...]
            r_ref[...] = _top2_rows(logits, n_experts)


def _norm_call(x, update=None, pre=None, router=None, pack_h=False):
    T, D = x.shape
    tr = min(256, T)
    row = pl.BlockSpec((tr, D), lambda i: (i, 0))
    vec = pl.BlockSpec((1, D), lambda i: (0, 0))
    args, in_specs, out_shape, out_specs = [x], [row], [], []
    if update is not None:
        upd, pw, g = update
        args += [upd, pw.reshape(1, D), g.reshape(1, D)]
        in_specs += [row, vec, vec]
        out_shape.append(jax.ShapeDtypeStruct((T, D), F32))
        out_specs.append(row)
    if pre is not None:
        w, sh, sc = pre
        args += [w.reshape(1, D), sh.reshape(1, D), sc.reshape(1, D)]
        in_specs += [vec, vec, vec]
        if pack_h:
            out_shape.append(jax.ShapeDtypeStruct((T, D // 2), jnp.uint32))
            out_specs.append(pl.BlockSpec((tr, D // 2), lambda i: (i, 0)))
        else:
            out_shape.append(jax.ShapeDtypeStruct((T, D), BF16))
            out_specs.append(row)
    n_experts = 0
    if router is not None:
        wr, br = router
        n_experts = wr.shape[1]
        wr_p = jnp.zeros((D, LANES), F32).at[:, :n_experts].set(wr)
        br_p = jnp.zeros((1, LANES), F32).at[0, :n_experts].set(br)
        args += [wr_p, br_p]
        in_specs += [pl.BlockSpec((D, LANES), lambda i: (0, 0)), pl.BlockSpec((1, LANES), lambda i: (0, 0))]
        out_shape.append(jax.ShapeDtypeStruct((T, LANES), F32))
        out_specs.append(pl.BlockSpec((tr, LANES), lambda i: (i, 0)))
    kern = functools.partial(_norm_kernel, has_update=update is not None, has_pre=pre is not None,
                             pack_h=pack_h, n_experts=n_experts)
    return pl.pallas_call(kern, grid=(T // tr,), in_specs=in_specs, out_specs=out_specs,
                          out_shape=out_shape, compiler_params=_params(1), name="norm")(*args)


def _mm_kernel(a_ref, b_ref, o_ref):
    o_ref[...] = _dot(a_ref[...], b_ref[...]).astype(o_ref.dtype)


def _mm_acc_kernel(a_ref, b_ref, o_ref, acc_ref, *, nk):
    k = pl.program_id(2)
    p = _dot(a_ref[...], b_ref[...])

    @pl.when(k == 0)
    def _():
        acc_ref[...] = p

    @pl.when(k > 0)
    def _():
        acc_ref[...] += p

    @pl.when(k == nk - 1)
    def _():
        o_ref[...] = acc_ref[...].astype(o_ref.dtype)


def _k_tile(K, limit=4096):
    if K <= limit:
        return K
    tk = limit - limit % 256
    while K % tk:
        tk -= 256
    return tk


def _matmul(a, b, out_dtype, tm=1024, tn=1024):
    M, K = a.shape
    N = b.shape[1]
    tm, tn, tk = _tile(M, tm), _tile(N, tn), _k_tile(K)
    out_shape = jax.ShapeDtypeStruct((M, N), out_dtype)
    if tk == K:
        return pl.pallas_call(
            _mm_kernel, grid=(M // tm, N // tn),
            in_specs=[pl.BlockSpec((tm, K), lambda i, j: (i, 0)), pl.BlockSpec((K, tn), lambda i, j: (0, j))],
            out_specs=pl.BlockSpec((tm, tn), lambda i, j: (i, j)),
            out_shape=out_shape, compiler_params=_params(2), name="matmul")(a, b)
    nk = K // tk
    return pl.pallas_call(
        functools.partial(_mm_acc_kernel, nk=nk), grid=(M // tm, N // tn, nk),
        in_specs=[pl.BlockSpec((tm, tk), lambda i, j, k: (i, k)), pl.BlockSpec((tk, tn), lambda i, j, k: (k, j))],
        out_specs=pl.BlockSpec((tm, tn), lambda i, j, k: (i, j)),
        out_shape=out_shape, scratch_shapes=[pltpu.VMEM((tm, tn), F32)],
        compiler_params=_params(3), name="matmul_acc")(a, b)


def _swiglu_kernel(a_ref, wg_ref, wu_ref, o_ref):
    a = a_ref[...]
    g = _dot(a, wg_ref[...])
    u = _dot(a, wu_ref[...])
    o_ref[...] = (g * jax.nn.sigmoid(g) * u).astype(o_ref.dtype)


def _swiglu_up(a, wg, wu, tm=1024, tn=512):
    M, K = a.shape
    N = wg.shape[1]
    tm, tn = _tile(M, tm), _tile(N, tn)
    wspec = pl.BlockSpec((K, tn), lambda i, j: (0, j))
    return pl.pallas_call(
        _swiglu_kernel, grid=(M // tm, N // tn),
        in_specs=[pl.BlockSpec((tm, K), lambda i, j: (i, 0)), wspec, wspec],
        out_specs=pl.BlockSpec((tm, tn), lambda i, j: (i, j)),
        out_shape=jax.ShapeDtypeStruct((M, N), BF16), compiler_params=_params(2), name="swiglu_up")(a, wg, wu)


def _merge_kernel(a1_ref, a2_ref, w1_ref, w2_ref, g1_ref, g2_ref, o_ref):
    y1 = _dot(a1_ref[...], w1_ref[...])
    y2 = _dot(a2_ref[...], w2_ref[...])
    g1 = jax.nn.sigmoid(g1_ref[...].astype(F32))
    g2 = jax.nn.sigmoid(g2_ref[...].astype(F32))
    o_ref[...] = (g1 * y1 + g2 * y2).astype(o_ref.dtype)


def _gated_merge(attn, gm, proj, ga_col, gb_col, w_pa, w_pb, tm=1024, tn=512):
    M, K1 = attn.shape
    K2 = gm.shape[1]
    N = w_pa.shape[1]
    tm, tn = _tile(M, tm), _tile(N, tn)
    ja, jb = ga_col // tn, gb_col // tn
    return pl.pallas_call(
        _merge_kernel, grid=(M // tm, N // tn),
        in_specs=[pl.BlockSpec((tm, K1), lambda i, j: (i, 0)),
                  pl.BlockSpec((tm, K2), lambda i, j: (i, 0)),
                  pl.BlockSpec((K1, tn), lambda i, j: (0, j)),
                  pl.BlockSpec((K2, tn), lambda i, j: (0, j)),
                  pl.BlockSpec((tm, tn), lambda i, j: (i, ja + j)),
                  pl.BlockSpec((tm, tn), lambda i, j: (i, jb + j))],
        out_specs=pl.BlockSpec((tm, tn), lambda i, j: (i, j)),
        out_shape=jax.ShapeDtypeStruct((M, N), BF16), compiler_params=_params(2), name="gated_merge",
    )(attn, gm, w_pa, w_pb, proj, proj)


def _prep_kernel(ql_ref, kl_ref, vl_ref, cos_ref, sin_ref, qc_ref, kc_ref, vc_ref,
                 qTl_ref, qTc_ref, ko_ref, vT_ref, amax_ref, *, heads, hd, qscale, n_lat):
    i = pl.program_id(0)
    hv = hd + ONES_ROWS
    row = lax.broadcasted_iota(jnp.int32, (8, hd), 0)

    @pl.when(i == 0)
    def _():
        amax_ref[...] = jnp.zeros(amax_ref.shape, F32)

    def emit(q_ref, k_ref, v_ref, rot, qT_ref, q_row):
        ones = jnp.ones((ONES_ROWS, q_ref.shape[0]), BF16)
        qmax = kmax = None
        for h in range(heads):
            sl = slice(h * hd, (h + 1) * hd)
            q = (rot(q_ref[:, sl].astype(F32)) * qscale).astype(BF16)
            k = rot(k_ref[:, sl].astype(F32)).astype(BF16)
            qT_ref[sl, :] = q.astype(F32).T.astype(BF16)
            ko_ref[:, sl] = k
            vT_ref[h * hv:h * hv + hd, :] = v_ref[:, sl].astype(F32).T.astype(BF16)
            vT_ref[h * hv + hd:(h + 1) * hv, :] = ones
            qa, ka = jnp.abs(q.astype(F32)), jnp.abs(k.astype(F32))
            qmax = qa if qmax is None else jnp.maximum(qmax, qa)
            kmax = ka if kmax is None else jnp.maximum(kmax, ka)
        blk = jnp.where(row == q_row, jnp.max(qmax, axis=0, keepdims=True),
                        jnp.where(row == 2, jnp.max(kmax, axis=0, keepdims=True), 0.0))
        amax_ref[...] = jnp.maximum(amax_ref[...], blk)

    @pl.when(i < n_lat)
    def _():
        cos, sin = cos_ref[...], sin_ref[...]
        lane = lax.broadcasted_iota(jnp.int32, cos.shape, 1)
        first = (lane % 32) < 16

        def rot(t):
            partner = jnp.where(first, pltpu.roll(t, hd - 16, 1), pltpu.roll(t, 16, 1))
            return t * cos + partner * sin

        emit(ql_ref, kl_ref, vl_ref, rot, qTl_ref, 0)

    @pl.when(i >= n_lat)
    def _():
        emit(qc_ref, kc_ref, vc_ref, lambda t: t, qTc_ref, 1)


def _prep(proj_l, proj_c, width, hd, qscale, tables):
    S, C = proj_l.shape[0], proj_c.shape[0]
    tr = min(256, S, C)
    n_lat, n_ctx = S // tr, C // tr
    heads = width // hd
    lat = lambda i: jnp.minimum(i, n_lat - 1)
    ctx = lambda i: jnp.maximum(i - n_lat, 0)
    col_l = lambda c: pl.BlockSpec((tr, width), lambda i: (lat(i), c))
    col_c = lambda c: pl.BlockSpec((tr, width), lambda i: (ctx(i), c))
    tab = pl.BlockSpec((tr, hd), lambda i: (lat(i), 0))
    vrows = heads * (hd + ONES_ROWS)
    return pl.pallas_call(
        functools.partial(_prep_kernel, heads=heads, hd=hd, qscale=qscale, n_lat=n_lat),
        grid=(n_lat + n_ctx,),
        in_specs=[col_l(0), col_l(1), col_l(2), tab, tab, col_c(0), col_c(1), col_c(2)],
        out_specs=[pl.BlockSpec((width, tr), lambda i: (0, lat(i))), pl.BlockSpec((width, tr), lambda i: (0, ctx(i))),
                   pl.BlockSpec((tr, width), lambda i: (i, 0)), pl.BlockSpec((vrows, tr), lambda i: (0, i)),
                   pl.BlockSpec((8, hd), lambda i: (0, 0))],
        out_shape=[jax.ShapeDtypeStruct((width, S), BF16), jax.ShapeDtypeStruct((width, C), BF16),
                   jax.ShapeDtypeStruct((S + C, width), BF16), jax.ShapeDtypeStruct((vrows, S + C), BF16),
                   jax.ShapeDtypeStruct((8, hd), F32)],
        compiler_params=_params(1), name="prep",
    )(proj_l, proj_l, proj_l, tables[0], tables[1], proj_c, proj_c, proj_c)


def _split8_kernel(sc_ref, k_ref, o_ref, *, heads, hd):
    half = hd // 2
    lane = lax.broadcasted_iota(jnp.int32, (k_ref.shape[0], hd), 1)
    low = lane < half
    for h in range(heads):
        x = k_ref[:, h * hd:(h + 1) * hd].astype(F32) * sc_ref[0]
        hi = x.astype(F8).astype(F32)
        lo = (x - hi).astype(F8).astype(F32)
        hi_r, lo_r = pltpu.roll(hi, half, 1), pltpu.roll(lo, half, 1)
        pieces = (jnp.where(low, hi, lo_r), jnp.where(low, hi, 0.0),
                  jnp.where(low, hi_r, lo), jnp.where(low, hi_r, 0.0))
        for i, piece in enumerate(pieces):
            o_ref[:, (4 * h + i) * hd:(4 * h + i + 1) * hd] = piece.astype(F8)


def _split8(k, scale, hd):
    T, width = k.shape
    tr = min(256, T)
    return pl.pallas_call(
        functools.partial(_split8_kernel, heads=width // hd, hd=hd),
        grid=(T // tr,),
        in_specs=[pl.BlockSpec(memory_space=pltpu.SMEM), pl.BlockSpec((tr, width), lambda i: (i, 0))],
        out_specs=pl.BlockSpec((tr, 4 * width), lambda i: (i, 0)),
        out_shape=jax.ShapeDtypeStruct((T, 4 * width), F8),
        compiler_params=_params(1), name="split8")(scale, k)


def _fp8_scale(amax):
    e = jnp.floor(jnp.log2(256.0 / jnp.maximum(amax, 1e-30)))
    return jnp.exp2(jnp.clip(e, -60.0, 60.0))


def _rope_tables(n_tokens, hd):
    freqs = hd // 8
    t = jnp.arange(n_tokens)
    inv = 1.0 / (ROPE_THETA ** (jnp.arange(freqs, dtype=F32) / freqs))
    ar = (t // GRID_W).astype(F32)[:, None] * inv
    ac = (t % GRID_W).astype(F32)[:, None] * inv
    cos = jnp.concatenate([jnp.cos(ar), jnp.cos(ar), jnp.cos(ac), jnp.cos(ac)] * 2, axis=1)
    sin = jnp.concatenate([-jnp.sin(ar), jnp.sin(ar), -jnp.sin(ac), jnp.sin(ac)] * 2, axis=1)
    return cos, sin


SIDE_STEPS = 8


def _attn_kernel(sc_ref, qT_ref, k_ref, vT_ref, sub_ref, *rest, n, tq, dk, out_scale, n_side):
    side_in, o_ref, side_out = rest[:n_side], rest[n_side], rest[n_side + 1:2 * n_side + 1]
    q8_ref, s_ref, mt_ref, m_ref, acc_ref = rest[2 * n_side + 1:]
    _attn_body(sc_ref, qT_ref, k_ref, vT_ref, sub_ref, o_ref, q8_ref, s_ref, mt_ref, m_ref, acc_ref,
               side_in, side_out, n=n, tq=tq, dk=dk, out_scale=out_scale)


def _attn_body(sc_ref, qT_ref, k_ref, vT_ref, sub_ref, o_ref, q8_ref, s_ref, mt_ref, m_ref, acc_ref,
               side_in, side_out, *, n, tq, dk, out_scale):
    j = pl.program_id(2)
    W = 2 * tq
    dv = 2 * dk
    kw = 4 * dk
    n_groups = W // MXU_COLS
    per_map = tq // MXU_COLS

    def scores(slot, c):
        cols = slice(c * MXU_COLS, (c + 1) * MXU_COLS)
        mp, qc = c // per_map, c % per_map
        s = _dot(k_ref[:, mp * kw:(mp + 1) * kw],
                 q8_ref[mp, :, qc * MXU_COLS:(qc + 1) * MXU_COLS]).astype(BF16)
        s_ref[slot, :, cols] = s
        mt_ref[slot, :, cols] = jnp.max(s, axis=0, keepdims=True).astype(F32)

    def consume(slot, c):
        cols = slice(c * MXU_COLS, (c + 1) * MXU_COLS)
        unscale = sc_ref[2]
        m_prev = m_ref[:, cols]
        m_new = jnp.maximum(m_prev, mt_ref[slot, :, cols])
        alpha = jnp.exp2((m_prev - m_new) * unscale)
        p = jnp.exp2((s_ref[slot, :, cols] - m_new.astype(BF16)) * unscale.astype(BF16))
        acc_ref[:, cols] = alpha * acc_ref[:, cols] + _dot(vT_ref[...], p)
        m_ref[:, cols] = m_new

    @pl.when(j == 0)
    def _first():
        q = qT_ref[...].astype(F32) * sc_ref[1]
        hi = q.astype(F8)
        lo = (q - hi.astype(F32)).astype(F8)
        for mp in range(2):
            rows = slice(mp * dk, (mp + 1) * dk)
            q8_ref[mp, 0:dk, :] = hi[rows]
            q8_ref[mp, dk:2 * dk, :] = hi[rows]
            q8_ref[mp, 2 * dk:3 * dk, :] = lo[rows]
            q8_ref[mp, 3 * dk:kw, :] = jnp.zeros((dk, tq), F8)
        m_ref[...] = jnp.full(m_ref.shape, -jnp.inf, F32)
        acc_ref[...] = jnp.zeros(acc_ref.shape, F32)
        for c in range(n_groups):
            scores(0, c)

    for parity in range(2):
        @pl.when((j > 0) & (j < n) & (j % 2 == parity))
        def _steady():
            for c in range(n_groups):
                scores(parity, c)
                consume(1 - parity, c)
            for a_ref, b_ref in zip(side_in, side_out):
                cols = a_ref.shape[1]
                b_ref[:, 0:cols] = a_ref[...].astype(b_ref.dtype)
                if b_ref.shape[1] > cols:
                    b_ref[:, cols:] = jnp.zeros((b_ref.shape[0], b_ref.shape[1] - cols), b_ref.dtype)

    @pl.when(j == n)
    def _last():
        for c in range(n_groups):
            consume((n - 1) % 2, c)
        o_both = acc_ref[0:dv, :] / acc_ref[dv:dv + 1, :]
        o = o_both[:, 0:tq] - sc_ref[0] * o_both[:, tq:W]
        y = o * lax.rsqrt(jnp.mean(o * o, axis=0, keepdims=True) + NORM_EPS)
        y = y * (sub_ref[...] * out_scale)
        o_ref[...] = y.T.astype(o_ref.dtype)


def _attention(lam, qT, qmax, k8, k_scale, vT, subln, out_scale, key0=0, n_keys=None, side=(), tq=2048, tk=1280):
    W, Sq = qT.shape
    hd = subln.shape[0]
    hv = hd + ONES_ROWS
    H = W // hd
    T = n_keys or k8.shape[0]
    tq, tk = _tile(Sq, tq), _tile(T, tk)
    n = T // tk
    nq = Sq // tq
    assert key0 % tk == 0
    kb0 = key0 // tk
    q_scale = _fp8_scale(qmax)
    scalars = jnp.stack([lam.reshape(()), q_scale, 1.0 / (q_scale * k_scale)]).astype(F32)
    plain_casts = []
    if n - 1 < SIDE_STEPS:
        plain_casts = [jnp.pad(w[r0:r0 + nr].astype(BF16), ((0, 0), (0, oc - w.shape[1]))) for w, oc, r0, nr in side]
        side = ()
    total = H * nq * SIDE_STEPS

    def chunk_step(h, i, j):
        return (h * nq + i) * SIDE_STEPS + jnp.clip(j - 1, 0, SIDE_STEPS - 1)

    side_in_specs, side_out_specs, side_shapes = [], [], []
    for w, out_cols, row0, rows in side:
        cols = w.shape[1]
        chunks = total
        while chunks > 1 and (total % chunks or rows % chunks or (rows // chunks) % 16):
            chunks -= 1
        assert rows % chunks == 0 and total % chunks == 0 and row0 % (rows // chunks) == 0
        stride, blk0 = total // chunks, row0 // (rows // chunks)
        omap = lambda h, i, j, stride=stride: (chunk_step(h, i, j) // stride, 0)
        imap = lambda h, i, j, stride=stride, blk0=blk0: (blk0 + chunk_step(h, i, j) // stride, 0)
        side_in_specs.append(pl.BlockSpec((rows // chunks, cols), imap))
        side_out_specs.append(pl.BlockSpec((rows // chunks, out_cols), omap))
        side_shapes.append(jax.ShapeDtypeStruct((rows, out_cols), BF16))
    outs = pl.pallas_call(
        functools.partial(_attn_kernel, n=n, tq=tq, dk=hd // 2, out_scale=out_scale, n_side=len(side)),
        grid=(H, nq, n + 1),
        in_specs=[pl.BlockSpec(memory_space=pltpu.SMEM),
                  pl.BlockSpec((hd, tq), lambda h, i, j: (h, i)),
                  pl.BlockSpec((tk, 4 * hd), lambda h, i, j: (kb0 + jnp.minimum(j, n - 1), h)),
                  pl.BlockSpec((hv, tk), lambda h, i, j: (h, kb0 + jnp.maximum(j - 1, 0))),
                  pl.BlockSpec((hd, 1), lambda h, i, j: (0, 0))] + side_in_specs,
        out_specs=[pl.BlockSpec((tq, hd), lambda h, i, j: (i, h))] + side_out_specs,
        out_shape=[jax.ShapeDtypeStruct((Sq, W), BF16)] + side_shapes,
        scratch_shapes=[pltpu.VMEM((2, 2 * hd, tq), F8), pltpu.VMEM((2, tk, 2 * tq), BF16),
                        pltpu.VMEM((2, 1, 2 * tq), F32), pltpu.VMEM((1, 2 * tq), F32),
                        pltpu.VMEM((hv, 2 * tq), F32)],
        compiler_params=_params(3), name="diff_attention",
    )(scalars, qT, k8, vT, subln.reshape(hd, 1), *[entry[0] for entry in side])
    return outs[0], list(outs[1:]) + plain_casts


def _gmlp_kernel(u_ref, v_ref, gw_ref, gb_ref, ws_ref, bs_ref, o_ref, *, groups, chunk, gdim):
    v = v_ref[...].astype(F32)
    vc = v - jnp.mean(v, axis=-1, keepdims=True)
    vn = vc * lax.rsqrt(jnp.mean(vc * vc, axis=-1, keepdims=True) + NORM_EPS)
    vn = (vn * gw_ref[...] + gb_ref[...]).astype(BF16)
    n_chunks = v.shape[0] // chunk
    for g in range(groups):
        cols = slice(g * gdim, (g + 1) * gdim)
        rhs = jnp.concatenate([vn[c * chunk:(c + 1) * chunk, cols] for c in range(n_chunks)], axis=1)
        mixed = _dot(ws_ref[g], rhs)
        for c in range(n_chunks):
            rows = slice(c * chunk, (c + 1) * chunk)
            m = mixed[:, c * gdim:(c + 1) * gdim] + bs_ref[g]
            o_ref[rows, cols] = (u_ref[rows, cols].astype(F32) * m).astype(o_ref.dtype)


def _spatial_gating(proj, u_col, v_col, gn_w, gn_b, w_s, b_s):
    T = proj.shape[0]
    G, chunk, _ = w_s.shape
    GW = gn_w.shape[0]
    gdim = GW // G
    tr = min(512, T)
    bias = jnp.broadcast_to(b_s[:, :, None], (G, chunk, gdim)).astype(F32)
    vec = pl.BlockSpec((1, GW), lambda i: (0, 0))
    return pl.pallas_call(
        functools.partial(_gmlp_kernel, groups=G, chunk=chunk, gdim=gdim),
        grid=(T // tr,),
        in_specs=[pl.BlockSpec((tr, GW), lambda i: (i, u_col // GW)),
                  pl.BlockSpec((tr, GW), lambda i: (i, v_col // GW)),
                  vec, vec,
                  pl.BlockSpec((G, chunk, chunk), lambda i: (0, 0, 0)),
                  pl.BlockSpec((G, chunk, gdim), lambda i: (0, 0, 0))],
        out_specs=pl.BlockSpec((tr, GW), lambda i: (i, 0)),
        out_shape=jax.ShapeDtypeStruct((T, GW), BF16),
        compiler_params=_params(1), name="spatial_gating",
    )(proj, proj, gn_w.reshape(1, GW), gn_b.reshape(1, GW), w_s.astype(BF16), bias)


DMA_UNROLL = 8


def _row_copy_loops(n_rows, copies):
    def start_all():
        def body(t, carry):
            for u in range(DMA_UNROLL):
                for c, d in enumerate(copies(t * DMA_UNROLL + u)):
                    d.start(priority=(u + c) % 2)
            return carry
        lax.fori_loop(0, n_rows // DMA_UNROLL, body, 0)

    def wait_all():
        def body(t, carry):
            for u in range(DMA_UNROLL):
                for d in copies(t * DMA_UNROLL + u):
                    d.wait()
            return carry
        lax.fori_loop(0, n_rows // DMA_UNROLL, body, 0)

    return start_all, wait_all


def _gather_rows_kernel(src_ref, nrows_ref, h_hbm, o_ref, buf_ref, sem, *, tg):
    i = pl.program_id(0)
    n_steps = pl.num_programs(0)

    def loops(step, slot):
        def copies(r):
            return [pltpu.make_async_copy(h_hbm.at[pl.ds(src_ref[step * tg + r], 1)],
                                          buf_ref.at[slot, pl.ds(r, 1)], sem.at[slot])]
        return _row_copy_loops(tg, copies)

    def used(step):
        return step * tg < nrows_ref[0]

    @pl.when((i == 0) & used(0))
    def _():
        loops(0, 0)[0]()

    for parity in range(2):
        @pl.when(i % 2 == parity)
        def _():
            @pl.when((i + 1 < n_steps) & used(i + 1))
            def _():
                loops(i + 1, 1 - parity)[0]()

            @pl.when(used(i))
            def _():
                loops(i, parity)[1]()
                o_ref[...] = buf_ref[parity]

            @pl.when(jnp.logical_not(used(i)))
            def _():
                o_ref[...] = jnp.zeros(o_ref.shape, o_ref.dtype)


def _gather_rows(h, src, n_rows, R, tg=256):
    W = h.shape[1]
    assert tg % DMA_UNROLL == 0
    return pl.pallas_call(
        functools.partial(_gather_rows_kernel, tg=tg),
        grid_spec=pltpu.PrefetchScalarGridSpec(
            num_scalar_prefetch=2, grid=(R // tg,),
            in_specs=[pl.BlockSpec(memory_space=pl.ANY)],
            out_specs=pl.BlockSpec((tg, W), lambda i, src, n: (i, 0)),
            scratch_shapes=[pltpu.VMEM((2, tg, W), h.dtype), pltpu.SemaphoreType.DMA((2,))]),
        out_shape=jax.ShapeDtypeStruct((R, W), h.dtype),
        compiler_params=_params(1), name="moe_gather")(src, n_rows, h)


def _unpack_bf16_pairs(w):
    lo = pltpu.bitcast(w << 16, F32).astype(BF16)
    hi = pltpu.bitcast(w & jnp.uint32(0xFFFF0000), F32).astype(BF16)
    return jnp.concatenate([lo, hi], axis=1)


def _moe_up_kernel(te_ref, nt_ref, x_ref, wg_ref, wu_ref, o_ref, xb_ref):
    t, j = pl.program_id(0), pl.program_id(1)

    @pl.when(t < nt_ref[0])
    def _():
        @pl.when(j == 0)
        def _():
            xb_ref[...] = _unpack_bf16_pairs(x_ref[...])

        a = xb_ref[...]
        g = _dot(a, wg_ref[0])
        u = _dot(a, wu_ref[0])
        o_ref[...] = (g * jax.nn.sigmoid(g) * u).astype(o_ref.dtype)

    @pl.when(t >= nt_ref[0])
    def _():
        o_ref[...] = jnp.zeros(o_ref.shape, o_ref.dtype)


def _moe_down_kernel(te_ref, nt_ref, a_ref, w_ref, o_ref):
    @pl.when(pl.program_id(0) < nt_ref[0])
    def _():
        o_ref[...] = _dot(a_ref[...], w_ref[0]).astype(o_ref.dtype)

    @pl.when(pl.program_id(0) >= nt_ref[0])
    def _():
        o_ref[...] = jnp.zeros(o_ref.shape, o_ref.dtype)


def _moe_experts(xs, tile_expert, n_tiles, wg, wu, wd, tm, tf=512, tn=1024):
    R = xs.shape[0]
    E, D, F = wg.shape
    NT = R // tm
    tf, tn = _tile(F, tf), _tile(D, tn)
    tile = lambda t, nt: jnp.minimum(t, nt[0] - 1)

    def wmap(last_j):
        return lambda t, j, te, nt: (te[t], 0, jnp.where(t < nt[0], j, last_j))

    hidden = pl.pallas_call(
        _moe_up_kernel,
        grid_spec=pltpu.PrefetchScalarGridSpec(
            num_scalar_prefetch=2, grid=(NT, F // tf),
            in_specs=[pl.BlockSpec((tm, D // 2), lambda t, j, te, nt: (tile(t, nt), 0)),
                      pl.BlockSpec((1, D, tf), wmap(F // tf - 1)),
                      pl.BlockSpec((1, D, tf), wmap(F // tf - 1))],
            out_specs=pl.BlockSpec((tm, tf), lambda t, j, te, nt: (t, j)),
            scratch_shapes=[pltpu.VMEM((tm, D), BF16)]),
        out_shape=jax.ShapeDtypeStruct((R, F), BF16),
        compiler_params=_params(2), name="moe_up")(tile_expert, n_tiles, xs, wg, wu)
    return pl.pallas_call(
        _moe_down_kernel,
        grid_spec=pltpu.PrefetchScalarGridSpec(
            num_scalar_prefetch=2, grid=(NT, D // tn),
            in_specs=[pl.BlockSpec((tm, F), lambda t, j, te, nt: (tile(t, nt), 0)),
                      pl.BlockSpec((1, F, tn), wmap(D // tn - 1))],
            out_specs=pl.BlockSpec((tm, tn), lambda t, j, te, nt: (t, j))),
        out_shape=jax.ShapeDtypeStruct((R, D), F32),
        compiler_params=_params(2), name="moe_down")(tile_expert, n_tiles, hidden, wd)


def _moe_combine_kernel(pos_ref, ys_hbm, x_ref, wt_ref, pw_ref, g_ref, o_ref, buf_ref, sem, *, tr):
    i = pl.program_id(0)
    n_steps = pl.num_programs(0)

    def loops(step, slot):
        def copies(r):
            return [pltpu.make_async_copy(ys_hbm.at[pl.ds(pos_ref[(step * tr + r) * TOP_K + kk], 1)],
                                          buf_ref.at[slot, kk, pl.ds(r, 1)], sem.at[slot])
                    for kk in range(TOP_K)]
        return _row_copy_loops(tr, copies)

    @pl.when(i == 0)
    def _():
        loops(0, 0)[0]()

    for parity in range(2):
        @pl.when(i % 2 == parity)
        def _():
            @pl.when(i + 1 < n_steps)
            def _():
                loops(i + 1, 1 - parity)[0]()

            loops(i, parity)[1]()
            wt = wt_ref[...]
            f = wt[:, 2:3] * buf_ref[parity, 0] + wt[:, 3:4] * buf_ref[parity, 1]
            fn = f * lax.rsqrt(jnp.mean(f * f, axis=-1, keepdims=True) + NORM_EPS)
            o_ref[...] = x_ref[...] + g_ref[...] * (fn * pw_ref[...])


def _moe_combine(pos, ys, x, route, post_w, gate, tr=128):
    T, D = x.shape
    tr = min(tr, T)
    assert tr % DMA_UNROLL == 0
    vec = pl.BlockSpec((1, D), lambda i, p: (0, 0))
    return pl.pallas_call(
        functools.partial(_moe_combine_kernel, tr=tr),
        grid_spec=pltpu.PrefetchScalarGridSpec(
            num_scalar_prefetch=1, grid=(T // tr,),
            in_specs=[pl.BlockSpec(memory_space=pl.ANY),
                      pl.BlockSpec((tr, D), lambda i, p: (i, 0)),
                      pl.BlockSpec((tr, LANES), lambda i, p: (i, 0)),
                      vec, vec],
            out_specs=pl.BlockSpec((tr, D), lambda i, p: (i, 0)),
            scratch_shapes=[pltpu.VMEM((2, TOP_K, tr, D), F32), pltpu.SemaphoreType.DMA((2,))]),
        out_shape=jax.ShapeDtypeStruct((T, D), F32),
        compiler_params=_params(1), name="moe_combine",
    )(pos, ys, x, route, post_w.reshape(1, D), gate.reshape(1, D))


def _moe_layer(x, h_packed, route, wg, wu, wd, post_w, gate, tm=512):
    T = x.shape[0]
    E = wg.shape[0]
    tm = min(tm, T)
    A = T * TOP_K
    R = _round_up(A, tm) + E * tm
    expert = route[:, :TOP_K].astype(jnp.int32).reshape(A)
    onehot = (expert[:, None] == jnp.arange(E, dtype=jnp.int32)[None, :]).astype(jnp.int32)
    csum = jnp.cumsum(onehot, axis=0)
    counts = csum[-1]
    rank = jnp.sum((csum - 1) * onehot, axis=1)
    padded = (counts + tm - 1) // tm * tm
    ends = jnp.cumsum(padded)
    offsets = ends - padded
    pos = (jnp.sum(onehot * offsets[None, :], axis=1) + rank).astype(jnp.int32)
    src = jnp.zeros((R,), jnp.int32).at[pos].set(jnp.arange(A, dtype=jnp.int32) // TOP_K)
    n_rows = ends[-1:].astype(jnp.int32)
    n_tiles = n_rows // tm
    tile_start = jnp.arange(R // tm, dtype=jnp.int32) * tm
    tile_expert = jnp.sum((tile_start[:, None] >= ends[None, :]).astype(jnp.int32), axis=1)
    tile_expert = jnp.minimum(tile_expert, jnp.max(jnp.where(counts > 0, jnp.arange(E), 0))).astype(jnp.int32)
    xs = _gather_rows(h_packed, src, n_rows, R)
    ys = _moe_experts(xs, tile_expert, n_tiles, wg, wu, wd, tm)
    return _moe_combine(pos, ys, x, route, post_w, gate)


def kernel(x, c, ctx, c_ctx, w_ada, b_ada, pre_norm_mix, post_norm_mix, pre_norm_ffn, post_norm_ffn, w_in, lambda_q1, lambda_k1, lambda_q2, lambda_k2, da_subln, gm_norm_w, gm_norm_b, gm_w_s, gm_b_s, w_branch_attn, w_branch_gmlp, w_out, ffn_w_gate, ffn_w_up, ffn_w_down, moe_w_router, moe_b_router, moe_w_gate, moe_w_up, moe_w_down):
    B, S, D = x.shape
    assert B == 1, "the kernels process one sequence"
    L = w_ada.shape[0]
    hd = da_subln.shape[1]
    dk = lambda_q1.shape[1]
    assert hd == 2 * dk == LANES
    qk_w = w_branch_attn.shape[1]
    gm_w = gm_norm_w.shape[1]
    u_col, v_col = 3 * qk_w, 3 * qk_w + gm_w
    ga_col, gb_col = 3 * qk_w + 2 * gm_w, 3 * qk_w + 2 * gm_w + D
    assert w_in.shape[2] == gb_col + D
    qscale = dk ** -0.5 * LOG2E

    xl, xc = x[0], ctx[0]
    cc = jnp.zeros((8, D), F32).at[0].set(c[0]).at[1].set(c_ctx)
    mod = _ada(cc, w_ada, b_ada)
    mod_l = mod[:, 0].reshape(L, 6, D)
    mod_c = mod[:, 1].reshape(L, 6, D)
    tables = _rope_tables(S, hd)

    hl = _norm_call(xl, pre=(pre_norm_mix[0], mod_l[0, 0], mod_l[0, 1]))[0]
    hc = _norm_call(xc, pre=(pre_norm_mix[0], mod_c[0, 0], mod_c[0, 1]))[0]

    next_casts = None
    for l in range(L):
        need_ctx = l < L - 1
        last = l == L - 1
        lambda_init = 0.8 - 0.6 * math.exp(-0.3 * l)
        lam = (jnp.exp(jnp.sum(lambda_q1[l] * lambda_k1[l])) - jnp.exp(jnp.sum(lambda_q2[l] * lambda_k2[l]))
               + lambda_init).reshape(1).astype(F32)
        if next_casts is None:
            w_pa, w_pb, w_o, w_i = (w[l].astype(BF16) for w in (w_branch_attn, w_branch_gmlp, w_out, w_in))
        else:
            w_pa, w_pb, w_o, w_i = next_casts
        out_scale = 1.0 - lambda_init
        is_moe = l % 2 == 1
        i = l // 2

        proj_l, proj_c = _matmul(hl, w_i, BF16), _matmul(hc, w_i, BF16)
        qT_l, qT_c, k_all, vT_all, amax = _prep(proj_l, proj_c, qk_w, hd, qscale, tables)
        k_scale = _fp8_scale(jnp.max(amax[2]))
        k8_all = _split8(k_all, k_scale.reshape(1), hd)
        def rows_of(stack, idx, out_cols=None):
            rows, cols = math.prod(stack.shape[1:-1]), stack.shape[-1]
            return (stack.reshape(-1, cols), out_cols or cols, idx * rows, rows)

        if is_moe:
            E, _, F = moe_w_gate.shape[1:]
            side = [rows_of(moe_w_gate, i), rows_of(moe_w_up, i), rows_of(moe_w_down, i)]
        else:
            FF = ffn_w_gate.shape[2]
            FFp = _round_up(FF, 1024)
            side = [rows_of(ffn_w_gate, i, FFp), rows_of(ffn_w_up, i, FFp)]
        if not last:
            side += [rows_of(w, l + 1) for w in (w_branch_attn, w_branch_gmlp, w_out, w_in)]
        attn_l, casts = _attention(lam, qT_l, jnp.max(amax[0]), k8_all, k_scale, vT_all,
                                   da_subln[l], out_scale, side=side)
        next_casts = None if last else casts[-4:]
        gm_l = _spatial_gating(proj_l, u_col, v_col, gm_norm_w[l], gm_norm_b[l], gm_w_s[l], gm_b_s[l])
        ol = _matmul(_gated_merge(attn_l, gm_l, proj_l, ga_col, gb_col, w_pa, w_pb), w_o, BF16)
        if need_ctx:
            attn_c, _ = _attention(lam, qT_c, jnp.max(amax[1]), k8_all, k_scale, vT_all, da_subln[l], out_scale,
                                   key0=S, n_keys=xc.shape[0])
            gm_c = _spatial_gating(proj_c, u_col, v_col, gm_norm_w[l], gm_norm_b[l], gm_w_s[l], gm_b_s[l])
            oc = _matmul(_gated_merge(attn_c, gm_c, proj_c, ga_col, gb_col, w_pa, w_pb), w_o, BF16)

        def mixer_update(xr, o, m):
            router = (moe_w_router[i], moe_b_router[i]) if is_moe else None
            return _norm_call(xr, update=(o, post_norm_mix[l], m[l, 2]),
                              pre=(pre_norm_ffn[l], m[l, 3], m[l, 4]), router=router, pack_h=is_moe)

        def next_pre(m):
            return None if last else (pre_norm_mix[l + 1], m[l + 1, 0], m[l + 1, 1])

        if is_moe:
            wg, wu, wd = casts[0].reshape(E, D, F), casts[1].reshape(E, D, F), casts[2].reshape(E, F, D)
            xl, hl2, route = mixer_update(xl, ol, mod_l)
            xl = _moe_layer(xl, hl2, route, wg, wu, wd, post_norm_ffn[l], mod_l[l, 5])
            if not last:
                hl = _norm_call(xl, pre=next_pre(mod_l))[0]
            if need_ctx:
                xc, hc2, route_c = mixer_update(xc, oc, mod_c)
                xc = _moe_layer(xc, hc2, route_c, wg, wu, wd, post_norm_ffn[l], mod_c[l, 5])
                hc = _norm_call(xc, pre=next_pre(mod_c))[0]
        else:
            wg, wu = casts[0], casts[1]
            wd = jnp.pad(ffn_w_down[i].astype(BF16), ((0, FFp - FF), (0, 0)))

            def dense(xr, o, m):
                xr, h2 = mixer_update(xr, o, m)
                f = _matmul(_swiglu_up(h2, wg, wu), wd, BF16)
                res = _norm_call(xr, update=(f, post_norm_ffn[l], m[l, 5]), pre=next_pre(m))
                return res[0], (res[1] if not last else None)

            xl, hl = dense(xl, ol, mod_l)
            if need_ctx:
                xc, hc = dense(xc, oc, mod_c)
            elif not last:
                hc = _norm_call(xc, pre=next_pre(mod_c))[0]
    return xl[None]
```

```python
import functools
import math

import jax
import jax.numpy as jnp
from jax import lax
from jax.experimental import pallas as pl
from jax.experimental.pallas import tpu as pltpu

F32 = jnp.float32
BF16 = jnp.bfloat16
F8 = jnp.float8_e4m3fn

GRID_W = 64
ROPE_THETA = 10000.0
NORM_EPS = 1e-6
TOP_K = 2
LANES = 128
MXU_COLS = 256
ONES_ROWS = 16
VMEM_LIMIT = 56 * 1024 * 1024
LOG2E = 1.4426950408889634


def _params(n_axes):
    return pltpu.CompilerParams(dimension_semantics=("arbitrary",) * n_axes,
                                vmem_limit_bytes=VMEM_LIMIT)


def _dot(a, b):
    return jnp.dot(a, b, preferred_element_type=F32)


def _round_up(n, m):
    return (n + m - 1) // m * m


def _tile(n, pref):
    if n <= pref:
        return n
    t = pref - pref % LANES
    while n % t:
        t -= LANES
    return t


def _ada_kernel(c_ref, w_ref, b_ref, o_ref):
    c = c_ref[...]
    s = (c * jax.nn.sigmoid(c)).astype(BF16)
    o_ref[0] = _dot(s, w_ref[0].astype(BF16)) + b_ref[0]


def _ada(cc, w_ada, b_ada):
    L, D, N = w_ada.shape
    R = cc.shape[0]
    tn = _tile(N, 512)
    return pl.pallas_call(
        _ada_kernel,
        grid=(L, N // tn),
        in_specs=[pl.BlockSpec((R, D), lambda l, j: (0, 0)),
                  pl.BlockSpec((1, D, tn), lambda l, j: (l, 0, j)),
                  pl.BlockSpec((1, 1, tn), lambda l, j: (l, 0, j))],
        out_specs=pl.BlockSpec((1, R, tn), lambda l, j: (l, 0, j)),
        out_shape=jax.ShapeDtypeStruct((L, R, N), F32),
        compiler_params=_params(2),
        name="ada",
    )(cc, w_ada, b_ada.reshape(L, 1, N))


def _top2_rows(logits, n_experts):
    lane = lax.broadcasted_iota(jnp.int32, logits.shape, 1)
    lane_f = lane.astype(F32)
    neg = jnp.float32(-jnp.inf)
    lg = jnp.where(lane < n_experts, logits, neg)
    v1 = jnp.max(lg, axis=-1, keepdims=True)
    i1 = jnp.min(jnp.where(lg == v1, lane_f, float(LANES)), axis=-1, keepdims=True)
    lg2 = jnp.where(lane_f == i1, neg, lg)
    v2 = jnp.max(lg2, axis=-1, keepdims=True)
    i2 = jnp.min(jnp.where(lg2 == v2, lane_f, float(LANES)), axis=-1, keepdims=True)
    e = jnp.exp(v2 - v1)
    w1 = 1.0 / (1.0 + e)
    w2 = e / (1.0 + e)
    return jnp.where(lane == 0, i1, jnp.where(lane == 1, i2, jnp.where(lane == 2, w1, jnp.where(lane == 3, w2, 0.0))))


def _norm_kernel(*refs, has_update, has_pre, pack_h, n_experts):
    it = iter(refs)
    x_ref = next(it)
    if has_update:
        u_ref, pw_ref, g_ref = next(it), next(it), next(it)
    if has_pre:
        w_ref, sh_ref, sc_ref = next(it), next(it), next(it)
    if n_experts:
        wr_ref, br_ref = next(it), next(it)
    if has_update:
        xo_ref = next(it)
    if has_pre:
        h_ref = next(it)
    if n_experts:
        r_ref = next(it)

    x = x_ref[...]
    if has_update:
        u = u_ref[...].astype(F32)
        un = u * lax.rsqrt(jnp.mean(u * u, axis=-1, keepdims=True) + NORM_EPS)
        x = x + g_ref[...] * (un * pw_ref[...])
        xo_ref[...] = x
    if has_pre:
        y = x * lax.rsqrt(jnp.mean(x * x, axis=-1, keepdims=True) + NORM_EPS)
        h = (y * w_ref[...]) * (1.0 + sc_ref[...]) + sh_ref[...]
        if pack_h:
            half = h.shape[1] // 2
            lo = pltpu.bitcast(h[:, :half].astype(BF16).astype(F32), jnp.uint32)
            hi = pltpu.bitcast(h[:, half:].astype(BF16).astype(F32), jnp.uint32)
            h_ref[...] = (lo >> 16) | (hi & jnp.uint32(0xFFFF0000))
        else:
            h_ref[...] = h.astype(BF16)
        if n_experts:
            wr = wr_ref[...]
            h_hi = h.astype(BF16)
            h_lo = (h - h_hi.astype(F32)).astype(BF16)
            w_hi = wr.astype(BF16)
            w_lo = (wr - w_hi.astype(F32)).astype(BF16)
            logits = _dot(h_hi, w_hi) + (_dot(h_hi, w_lo) + _dot(h_lo, w_hi)) + br_ref[...]
            r_ref[...] = _top2_rows(logits, n_experts)


def _norm_call(x, update=None, pre=None, router=None, pack_h=False):
    T, D = x.shape
    tr = min(256, T)
    row = pl.BlockSpec((tr, D), lambda i: (i, 0))
    vec = pl.BlockSpec((1, D), lambda i: (0, 0))
    args, in_specs, out_shape, out_specs = [x], [row], [], []
    if update is not None:
        upd, pw, g = update
        args += [upd, pw.reshape(1, D), g.reshape(1, D)]
        in_specs += [row, vec, vec]
        out_shape.append(jax.ShapeDtypeStruct((T, D), F32))
        out_specs.append(row)
    if pre is not None:
        w, sh, sc = pre
        args += [w.reshape(1, D), sh.reshape(1, D), sc.reshape(1, D)]
        in_specs += [vec, vec, vec]
        if pack_h:
            out_shape.append(jax.ShapeDtypeStruct((T, D // 2), jnp.uint32))
            out_specs.append(pl.BlockSpec((tr, D // 2), lambda i: (i, 0)))
        else:
            out_shape.append(jax.ShapeDtypeStruct((T, D), BF16))
            out_specs.append(row)
    n_experts = 0
    if router is not None:
        wr, br = router
        n_experts = wr.shape[1]
        wr_p = jnp.zeros((D, LANES), F32).at[:, :n_experts].set(wr)
        br_p = jnp.zeros((1, LANES), F32).at[0, :n_experts].set(br)
        args += [wr_p, br_p]
        in_specs += [pl.BlockSpec((D, LANES), lambda i: (0, 0)), pl.BlockSpec((1, LANES), lambda i: (0, 0))]
        out_shape.append(jax.ShapeDtypeStruct((T, LANES), F32))
        out_specs.append(pl.BlockSpec((tr, LANES), lambda i: (i, 0)))
    kern = functools.partial(_norm_kernel, has_update=update is not None, has_pre=pre is not None,
                             pack_h=pack_h, n_experts=n_experts)
    return pl.pallas_call(kern, grid=(T // tr,), in_specs=in_specs, out_specs=out_specs,
                          out_shape=out_shape, compiler_params=_params(1), name="norm")(*args)


def _mm_kernel(a_ref, b_ref, o_ref):
    o_ref[...] = _dot(a_ref[...], b_ref[...]).astype(o_ref.dtype)


def _mm_acc_kernel(a_ref, b_ref, o_ref, acc_ref, *, nk):
    k = pl.program_id(2)
    p = _dot(a_ref[...], b_ref[...])

    @pl.when(k == 0)
    def _():
        acc_ref[...] = p

    @pl.when(k > 0)
    def _():
        acc_ref[...] += p

    @pl.when(k == nk - 1)
    def _():
        o_ref[...] = acc_ref[...].astype(o_ref.dtype)


def _k_tile(K, limit=4096):
    if K <= limit:
        return K
    tk = limit - limit % 256
    while K % tk:
        tk -= 256
    return tk


def _matmul(a, b, out_dtype, tm=1024, tn=1024):
    M, K = a.shape
    N = b.shape[1]
    tm, tn, tk = _tile(M, tm), _tile(N, tn), _k_tile(K)
    out_shape = jax.ShapeDtypeStruct((M, N), out_dtype)
    if tk == K:
        return pl.pallas_call(
            _mm_kernel, grid=(M // tm, N // tn),
            in_specs=[pl.BlockSpec((tm, K), lambda i, j: (i, 0)), pl.BlockSpec((K, tn), lambda i, j: (0, j))],
            out_specs=pl.BlockSpec((tm, tn), lambda i, j: (i, j)),
            out_shape=out_shape, compiler_params=_params(2), name="matmul")(a, b)
    nk = K // tk
    return pl.pallas_call(
        functools.partial(_mm_acc_kernel, nk=nk), grid=(M // tm, N // tn, nk),
        in_specs=[pl.BlockSpec((tm, tk), lambda i, j, k: (i, k)), pl.BlockSpec((tk, tn), lambda i, j, k: (k, j))],
        out_specs=pl.BlockSpec((tm, tn), lambda i, j, k: (i, j)),
        out_shape=out_shape, scratch_shapes=[pltpu.VMEM((tm, tn), F32)],
        compiler_params=_params(3), name="matmul_acc")(a, b)


def _swiglu_kernel(a_ref, wg_ref, wu_ref, o_ref):
    a = a_ref[...]
    g = _dot(a, wg_ref[...])
    u = _dot(a, wu_ref[...])
    o_ref[...] = (g * jax.nn.sigmoid(g) * u).astype(o_ref.dtype)


def _swiglu_up(a, wg, wu, tm=1024, tn=512):
    M, K = a.shape
    N = wg.shape[1]
    tm, tn = _tile(M, tm), _tile(N, tn)
    wspec = pl.BlockSpec((K, tn), lambda i, j: (0, j))
    return pl.pallas_call(
        _swiglu_kernel, grid=(M // tm, N // tn),
        in_specs=[pl.BlockSpec((tm, K), lambda i, j: (i, 0)), wspec, wspec],
        out_specs=pl.BlockSpec((tm, tn), lambda i, j: (i, j)),
        out_shape=jax.ShapeDtypeStruct((M, N), BF16), compiler_params=_params(2), name="swiglu_up")(a, wg, wu)


def _merge_kernel(a1_ref, a2_ref, w1_ref, w2_ref, g1_ref, g2_ref, o_ref):
    y1 = _dot(a1_ref[...], w1_ref[...])
    y2 = _dot(a2_ref[...], w2_ref[...])
    g1 = jax.nn.sigmoid(g1_ref[...].astype(F32))
    g2 = jax.nn.sigmoid(g2_ref[...].astype(F32))
    o_ref[...] = (g1 * y1 + g2 * y2).astype(o_ref.dtype)


def _gated_merge(attn, gm, proj, ga_col, gb_col, w_pa, w_pb, tm=1024, tn=512):
    M, K1 = attn.shape
    K2 = gm.shape[1]
    N = w_pa.shape[1]
    tm, tn = _tile(M, tm), _tile(N, tn)
    ja, jb = ga_col // tn, gb_col // tn
    return pl.pallas_call(
        _merge_kernel, grid=(M // tm, N // tn),
        in_specs=[pl.BlockSpec((tm, K1), lambda i, j: (i, 0)),
                  pl.BlockSpec((tm, K2), lambda i, j: (i, 0)),
                  pl.BlockSpec((K1, tn), lambda i, j: (0, j)),
                  pl.BlockSpec((K2, tn), lambda i, j: (0, j)),
                  pl.BlockSpec((tm, tn), lambda i, j: (i, ja + j)),
                  pl.BlockSpec((tm, tn), lambda i, j: (i, jb + j))],
        out_specs=pl.BlockSpec((tm, tn), lambda i, j: (i, j)),
        out_shape=jax.ShapeDtypeStruct((M, N), BF16), compiler_params=_params(2), name="gated_merge",
    )(attn, gm, w_pa, w_pb, proj, proj)


def _prep_kernel(ql_ref, kl_ref, vl_ref, cos_ref, sin_ref, qc_ref, kc_ref, vc_ref,
                 qTl_ref, qTc_ref, ko_ref, vT_ref, amax_ref, *, heads, hd, qscale, n_lat):
    i = pl.program_id(0)
    hv = hd + ONES_ROWS
    row = lax.broadcasted_iota(jnp.int32, (8, hd), 0)

    @pl.when(i == 0)
    def _():
        amax_ref[...] = jnp.zeros(amax_ref.shape, F32)

    def emit(q_ref, k_ref, v_ref, rot, qT_ref, q_row):
        ones = jnp.ones((ONES_ROWS, q_ref.shape[0]), BF16)
        qmax = kmax = None
        for h in range(heads):
            sl = slice(h * hd, (h + 1) * hd)
            q = (rot(q_ref[:, sl].astype(F32)) * qscale).astype(BF16)
            k = rot(k_ref[:, sl].astype(F32)).astype(BF16)
            qT_ref[sl, :] = q.astype(F32).T.astype(BF16)
            ko_ref[:, sl] = k
            vT_ref[h * hv:h * hv + hd, :] = v_ref[:, sl].astype(F32).T.astype(BF16)
            vT_ref[h * hv + hd:(h + 1) * hv, :] = ones
            qa, ka = jnp.abs(q.astype(F32)), jnp.abs(k.astype(F32))
            qmax = qa if qmax is None else jnp.maximum(qmax, qa)
            kmax = ka if kmax is None else jnp.maximum(kmax, ka)
        blk = jnp.where(row == q_row, jnp.max(qmax, axis=0, keepdims=True),
                        jnp.where(row == 2, jnp.max(kmax, axis=0, keepdims=True), 0.0))
        amax_ref[...] = jnp.maximum(amax_ref[...], blk)

    @pl.when(i < n_lat)
    def _():
        cos, sin = cos_ref[...], sin_ref[...]
        lane = lax.broadcasted_iota(jnp.int32, cos.shape, 1)
        first = (lane % 32) < 16

        def rot(t):
            partner = jnp.where(first, pltpu.roll(t, hd - 16, 1), pltpu.roll(t, 16, 1))
            return t * cos + partner * sin

        emit(ql_ref, kl_ref, vl_ref, rot, qTl_ref, 0)

    @pl.when(i >= n_lat)
    def _():
        emit(qc_ref, kc_ref, vc_ref, lambda t: t, qTc_ref, 1)


def _prep(proj_l, proj_c, width, hd, qscale, tables):
    S, C = proj_l.shape[0], proj_c.shape[0]
    tr = min(256, S, C)
    n_lat, n_ctx = S // tr, C // tr
    heads = width // hd
    lat = lambda i: jnp.minimum(i, n_lat - 1)
    ctx = lambda i: jnp.maximum(i - n_lat, 0)
    col_l = lambda c: pl.BlockSpec((tr, width), lambda i: (lat(i), c))
    col_c = lambda c: pl.BlockSpec((tr, width), lambda i: (ctx(i), c))
    tab = pl.BlockSpec((tr, hd), lambda i: (lat(i), 0))
    vrows = heads * (hd + ONES_ROWS)
    return pl.pallas_call(
        functools.partial(_prep_kernel, heads=heads, hd=hd, qscale=qscale, n_lat=n_lat),
        grid=(n_lat + n_ctx,),
        in_specs=[col_l(0), col_l(1), col_l(2), tab, tab, col_c(0), col_c(1), col_c(2)],
        out_specs=[pl.BlockSpec((width, tr), lambda i: (0, lat(i))), pl.BlockSpec((width, tr), lambda i: (0, ctx(i))),
                   pl.BlockSpec((tr, width), lambda i: (i, 0)), pl.BlockSpec((vrows, tr), lambda i: (0, i)),
                   pl.BlockSpec((8, hd), lambda i: (0, 0))],
        out_shape=[jax.ShapeDtypeStruct((width, S), BF16), jax.ShapeDtypeStruct((width, C), BF16),
                   jax.ShapeDtypeStruct((S + C, width), BF16), jax.ShapeDtypeStruct((vrows, S + C), BF16),
                   jax.ShapeDtypeStruct((8, hd), F32)],
        compiler_params=_params(1), name="prep",
    )(proj_l, proj_l, proj_l, tables[0], tables[1], proj_c, proj_c, proj_c)


def _split8_kernel(sc_ref, k_ref, o_ref, *, heads, hd):
    half = hd // 2
    lane = lax.broadcasted_iota(jnp.int32, (k_ref.shape[0], hd), 1)
    low = lane < half
    for h in range(heads):
        x = k_ref[:, h * hd:(h + 1) * hd].astype(F32) * sc_ref[0]
        hi = x.astype(F8).astype(F32)
        lo = (x - hi).astype(F8).astype(F32)
        hi_r, lo_r = pltpu.roll(hi, half, 1), pltpu.roll(lo, half, 1)
        pieces = (jnp.where(low, hi, lo_r), jnp.where(low, hi, 0.0),
                  jnp.where(low, hi_r, lo), jnp.where(low, hi_r, 0.0))
        for i, piece in enumerate(pieces):
            o_ref[:, (4 * h + i) * hd:(4 * h + i + 1) * hd] = piece.astype(F8)


def _split8(k, scale, hd):
    T, width = k.shape
    tr = min(256, T)
    return pl.pallas_call(
        functools.partial(_split8_kernel, heads=width // hd, hd=hd),
        grid=(T // tr,),
        in_specs=[pl.BlockSpec(memory_space=pltpu.SMEM), pl.BlockSpec((tr, width), lambda i: (i, 0))],
        out_specs=pl.BlockSpec((tr, 4 * width), lambda i: (i, 0)),
        out_shape=jax.ShapeDtypeStruct((T, 4 * width), F8),
        compiler_params=_params(1), name="split8")(scale, k)


def _fp8_scale(amax):
    e = jnp.floor(jnp.log2(256.0 / jnp.maximum(amax, 1e-30)))
    return jnp.exp2(jnp.clip(e, -60.0, 60.0))


def _rope_tables(n_tokens, hd):
    freqs = hd // 8
    lane = jnp.arange(hd)
    inv = 1.0 / (ROPE_THETA ** ((lane % freqs).astype(F32) / freqs))
    col_axis = (lane % (4 * freqs)) >= 2 * freqs
    sign = jnp.where(lane % (2 * freqs) < freqs, -1.0, 1.0)
    t = jnp.arange(n_tokens)[:, None]
    ang = jnp.where(col_axis, t % GRID_W, t // GRID_W).astype(F32) * inv
    return jnp.cos(ang), sign * jnp.sin(ang)


SIDE_STEPS = 8


def _attn_kernel(sc_ref, qT_ref, k_ref, vT_ref, sub_ref, *rest, n, tq, dk, out_scale, n_side):
    side_in, o_ref, side_out = rest[:n_side], rest[n_side], rest[n_side + 1:2 * n_side + 1]
    q8_ref, s_ref, mt_ref, m_ref, acc_ref = rest[2 * n_side + 1:]
    _attn_body(sc_ref, qT_ref, k_ref, vT_ref, sub_ref, o_ref, q8_ref, s_ref, mt_ref, m_ref, acc_ref,
               side_in, side_out, n=n, tq=tq, dk=dk, out_scale=out_scale)


def _attn_body(sc_ref, qT_ref, k_ref, vT_ref, sub_ref, o_ref, q8_ref, s_ref, mt_ref, m_ref, acc_ref,
               side_in, side_out, *, n, tq, dk, out_scale):
    j = pl.program_id(2)
    W = 2 * tq
    dv = 2 * dk
    kw = 4 * dk
    n_groups = W // MXU_COLS
    per_map = tq // MXU_COLS

    def scores(slot, c):
        cols = slice(c * MXU_COLS, (c + 1) * MXU_COLS)
        mp, qc = c // per_map, c % per_map
        s = _dot(k_ref[:, mp * kw:(mp + 1) * kw],
                 q8_ref[mp, :, qc * MXU_COLS:(qc + 1) * MXU_COLS]).astype(BF16)
        s_ref[slot, :, cols] = s
        mt_ref[slot, :, cols] = jnp.max(s, axis=0, keepdims=True).astype(F32)

    def consume(slot, c):
        cols = slice(c * MXU_COLS, (c + 1) * MXU_COLS)
        unscale = sc_ref[2]
        m_prev = m_ref[:, cols]
        m_new = jnp.maximum(m_prev, mt_ref[slot, :, cols])
        alpha = jnp.exp2((m_prev - m_new) * unscale)
        p = jnp.exp2((s_ref[slot, :, cols] - m_new.astype(BF16)) * unscale.astype(BF16))
        acc_ref[:, cols] = alpha * acc_ref[:, cols] + _dot(vT_ref[...], p)
        m_ref[:, cols] = m_new

    @pl.when(j == 0)
    def _first():
        q = qT_ref[...].astype(F32) * sc_ref[1]
        hi = q.astype(F8)
        lo = (q - hi.astype(F32)).astype(F8)
        for mp in range(2):
            rows = slice(mp * dk, (mp + 1) * dk)
            q8_ref[mp, 0:dk, :] = hi[rows]
            q8_ref[mp, dk:2 * dk, :] = hi[rows]
            q8_ref[mp, 2 * dk:3 * dk, :] = lo[rows]
            q8_ref[mp, 3 * dk:kw, :] = jnp.zeros((dk, tq), F8)
        m_ref[...] = jnp.full(m_ref.shape, -jnp.inf, F32)
        acc_ref[...] = jnp.zeros(acc_ref.shape, F32)
        for c in range(n_groups):
            scores(0, c)

    for parity in range(2):
        @pl.when((j > 0) & (j < n) & (j % 2 == parity))
        def _steady():
            for c in range(n_groups):
                scores(parity, c)
                consume(1 - parity, c)
            for a_ref, b_ref in zip(side_in, side_out):
                cols = a_ref.shape[1]
                b_ref[:, 0:cols] = a_ref[...].astype(b_ref.dtype)
                if b_ref.shape[1] > cols:
                    b_ref[:, cols:] = jnp.zeros((b_ref.shape[0], b_ref.shape[1] - cols), b_ref.dtype)

    @pl.when(j == n)
    def _last():
        for c in range(n_groups):
            consume((n - 1) % 2, c)
        o_both = acc_ref[0:dv, :] / acc_ref[dv:dv + 1, :]
        o = o_both[:, 0:tq] - sc_ref[0] * o_both[:, tq:W]
        y = o * lax.rsqrt(jnp.mean(o * o, axis=0, keepdims=True) + NORM_EPS)
        y = y * (sub_ref[...] * out_scale)
        o_ref[...] = y.T.astype(o_ref.dtype)


def _attention(lam, qT, qmax, k8, k_scale, vT, subln, out_scale, key0=0, n_keys=None, side=(), tq=2048, tk=1280):
    W, Sq = qT.shape
    hd = subln.shape[0]
    hv = hd + ONES_ROWS
    H = W // hd
    T = n_keys or k8.shape[0]
    tq, tk = _tile(Sq, tq), _tile(T, tk)
    n = T // tk
    nq = Sq // tq
    assert key0 % tk == 0
    kb0 = key0 // tk
    q_scale = _fp8_scale(qmax)
    scalars = jnp.stack([lam.reshape(()), q_scale, 1.0 / (q_scale * k_scale)]).astype(F32)
    plain_casts = []
    if n - 1 < SIDE_STEPS:
        plain_casts = [jnp.pad(w[r0:r0 + nr].astype(BF16), ((0, 0), (0, oc - w.shape[1]))) for w, oc, r0, nr in side]
        side = ()
    total = H * nq * SIDE_STEPS

    def chunk_step(h, i, j):
        return (h * nq + i) * SIDE_STEPS + jnp.clip(j - 1, 0, SIDE_STEPS - 1)

    side_in_specs, side_out_specs, side_shapes = [], [], []
    for w, out_cols, row0, rows in side:
        cols = w.shape[1]
        chunks = total
        while chunks > 1 and (total % chunks or rows % chunks or (rows // chunks) % 16):
            chunks -= 1
        assert rows % chunks == 0 and total % chunks == 0 and row0 % (rows // chunks) == 0
        stride, blk0 = total // chunks, row0 // (rows // chunks)
        omap = lambda h, i, j, stride=stride: (chunk_step(h, i, j) // stride, 0)
        imap = lambda h, i, j, stride=stride, blk0=blk0: (blk0 + chunk_step(h, i, j) // stride, 0)
        side_in_specs.append(pl.BlockSpec((rows // chunks, cols), imap))
        side_out_specs.append(pl.BlockSpec((rows // chunks, out_cols), omap))
        side_shapes.append(jax.ShapeDtypeStruct((rows, out_cols), BF16))
    outs = pl.pallas_call(
        functools.partial(_attn_kernel, n=n, tq=tq, dk=hd // 2, out_scale=out_scale, n_side=len(side)),
        grid=(H, nq, n + 1),
        in_specs=[pl.BlockSpec(memory_space=pltpu.SMEM),
                  pl.BlockSpec((hd, tq), lambda h, i, j: (h, i)),
                  pl.BlockSpec((tk, 4 * hd), lambda h, i, j: (kb0 + jnp.minimum(j, n - 1), h)),
                  pl.BlockSpec((hv, tk), lambda h, i, j: (h, kb0 + jnp.maximum(j - 1, 0))),
                  pl.BlockSpec((hd, 1), lambda h, i, j: (0, 0))] + side_in_specs,
        out_specs=[pl.BlockSpec((tq, hd), lambda h, i, j: (i, h))] + side_out_specs,
        out_shape=[jax.ShapeDtypeStruct((Sq, W), BF16)] + side_shapes,
        scratch_shapes=[pltpu.VMEM((2, 2 * hd, tq), F8), pltpu.VMEM((2, tk, 2 * tq), BF16),
                        pltpu.VMEM((2, 1, 2 * tq), F32), pltpu.VMEM((1, 2 * tq), F32),
                        pltpu.VMEM((hv, 2 * tq), F32)],
        compiler_params=_params(3), name="diff_attention",
    )(scalars, qT, k8, vT, subln.reshape(hd, 1), *[entry[0] for entry in side])
    return outs[0], list(outs[1:]) + plain_casts


def _gmlp_kernel(u_ref, v_ref, gw_ref, gb_ref, ws_ref, bs_ref, o_ref, *, groups, chunk, gdim):
    v = v_ref[...].astype(F32)
    vc = v - jnp.mean(v, axis=-1, keepdims=True)
    vn = vc * lax.rsqrt(jnp.mean(vc * vc, axis=-1, keepdims=True) + NORM_EPS)
    vn = (vn * gw_ref[...] + gb_ref[...]).astype(BF16)
    n_chunks = v.shape[0] // chunk
    for g in range(groups):
        cols = slice(g * gdim, (g + 1) * gdim)
        rhs = jnp.concatenate([vn[c * chunk:(c + 1) * chunk, cols] for c in range(n_chunks)], axis=1)
        mixed = _dot(ws_ref[g], rhs)
        for c in range(n_chunks):
            rows = slice(c * chunk, (c + 1) * chunk)
            m = mixed[:, c * gdim:(c + 1) * gdim] + bs_ref[g]
            o_ref[rows, cols] = (u_ref[rows, cols].astype(F32) * m).astype(o_ref.dtype)


def _spatial_gating(proj, u_col, v_col, gn_w, gn_b, w_s, b_s):
    T = proj.shape[0]
    G, chunk, _ = w_s.shape
    GW = gn_w.shape[0]
    gdim = GW // G
    tr = min(512, T)
    bias = jnp.broadcast_to(b_s[:, :, None], (G, chunk, gdim)).astype(F32)
    vec = pl.BlockSpec((1, GW), lambda i: (0, 0))
    return pl.pallas_call(
        functools.partial(_gmlp_kernel, groups=G, chunk=chunk, gdim=gdim),
        grid=(T // tr,),
        in_specs=[pl.BlockSpec((tr, GW), lambda i: (i, u_col // GW)),
                  pl.BlockSpec((tr, GW), lambda i: (i, v_col // GW)),
                  vec, vec,
                  pl.BlockSpec((G, chunk, chunk), lambda i: (0, 0, 0)),
                  pl.BlockSpec((G, chunk, gdim), lambda i: (0, 0, 0))],
        out_specs=pl.BlockSpec((tr, GW), lambda i: (i, 0)),
        out_shape=jax.ShapeDtypeStruct((T, GW), BF16),
        compiler_params=_params(1), name="spatial_gating",
    )(proj, proj, gn_w.reshape(1, GW), gn_b.reshape(1, GW), w_s.astype(BF16), bias)


DMA_UNROLL = 8


def _row_copy_loops(n_rows, copies):
    def start_all():
        def body(t, carry):
            for u in range(DMA_UNROLL):
                for c, d in enumerate(copies(t * DMA_UNROLL + u)):
                    d.start(priority=(u + c) % 2)
            return carry
        lax.fori_loop(0, n_rows // DMA_UNROLL, body, 0)

    def wait_all():
        def body(t, carry):
            for u in range(DMA_UNROLL):
                for d in copies(t * DMA_UNROLL + u):
                    d.wait()
            return carry
        lax.fori_loop(0, n_rows // DMA_UNROLL, body, 0)

    return start_all, wait_all


def _gather_rows_kernel(src_ref, nrows_ref, h_hbm, o_ref, buf_ref, sem, *, tg):
    i = pl.program_id(0)
    n_steps = pl.num_programs(0)

    def loops(step, slot):
        def copies(r):
            return [pltpu.make_async_copy(h_hbm.at[pl.ds(src_ref[step * tg + r], 1)],
                                          buf_ref.at[slot, pl.ds(r, 1)], sem.at[slot])]
        return _row_copy_loops(tg, copies)

    def used(step):
        return step * tg < nrows_ref[0]

    @pl.when((i == 0) & used(0))
    def _():
        loops(0, 0)[0]()

    for parity in range(2):
        @pl.when(i % 2 == parity)
        def _():
            @pl.when((i + 1 < n_steps) & used(i + 1))
            def _():
                loops(i + 1, 1 - parity)[0]()

            @pl.when(used(i))
            def _():
                loops(i, parity)[1]()
                o_ref[...] = buf_ref[parity]

            @pl.when(jnp.logical_not(used(i)))
            def _():
                o_ref[...] = jnp.zeros(o_ref.shape, o_ref.dtype)


def _gather_rows(h, src, n_rows, R, tg=256):
    W = h.shape[1]
    assert tg % DMA_UNROLL == 0
    return pl.pallas_call(
        functools.partial(_gather_rows_kernel, tg=tg),
        grid_spec=pltpu.PrefetchScalarGridSpec(
            num_scalar_prefetch=2, grid=(R // tg,),
            in_specs=[pl.BlockSpec(memory_space=pl.ANY)],
            out_specs=pl.BlockSpec((tg, W), lambda i, src, n: (i, 0)),
            scratch_shapes=[pltpu.VMEM((2, tg, W), h.dtype), pltpu.SemaphoreType.DMA((2,))]),
        out_shape=jax.ShapeDtypeStruct((R, W), h.dtype),
        compiler_params=_params(1), name="moe_gather")(src, n_rows, h)


def _unpack_bf16_pairs(w):
    lo = pltpu.bitcast(w << 16, F32).astype(BF16)
    hi = pltpu.bitcast(w & jnp.uint32(0xFFFF0000), F32).astype(BF16)
    return jnp.concatenate([lo, hi], axis=1)


def _moe_up_kernel(te_ref, nt_ref, x_ref, wg_ref, wu_ref, o_ref, xb_ref):
    t, j = pl.program_id(0), pl.program_id(1)

    @pl.when(t < nt_ref[0])
    def _():
        @pl.when(j == 0)
        def _():
            xb_ref[...] = _unpack_bf16_pairs(x_ref[...])

        a = xb_ref[...]
        g = _dot(a, wg_ref[0])
        u = _dot(a, wu_ref[0])
        o_ref[...] = (g * jax.nn.sigmoid(g) * u).astype(o_ref.dtype)

    @pl.when(t >= nt_ref[0])
    def _():
        o_ref[...] = jnp.zeros(o_ref.shape, o_ref.dtype)


def _moe_down_kernel(te_ref, nt_ref, a_ref, w_ref, o_ref):
    @pl.when(pl.program_id(0) < nt_ref[0])
    def _():
        o_ref[...] = _dot(a_ref[...], w_ref[0]).astype(o_ref.dtype)

    @pl.when(pl.program_id(0) >= nt_ref[0])
    def _():
        o_ref[...] = jnp.zeros(o_ref.shape, o_ref.dtype)


def _moe_experts(xs, tile_expert, n_tiles, wg, wu, wd, tm, tf=512, tn=1024):
    R = xs.shape[0]
    E, D, F = wg.shape
    NT = R // tm
    tf, tn = _tile(F, tf), _tile(D, tn)
    tile = lambda t, nt: jnp.minimum(t, nt[0] - 1)

    def wmap(last_j):
        return lambda t, j, te, nt: (te[t], 0, jnp.where(t < nt[0], j, last_j))

    hidden = pl.pallas_call(
        _moe_up_kernel,
        grid_spec=pltpu.PrefetchScalarGridSpec(
            num_scalar_prefetch=2, grid=(NT, F // tf),
            in_specs=[pl.BlockSpec((tm, D // 2), lambda t, j, te, nt: (tile(t, nt), 0)),
                      pl.BlockSpec((1, D, tf), wmap(F // tf - 1)),
                      pl.BlockSpec((1, D, tf), wmap(F // tf - 1))],
            out_specs=pl.BlockSpec((tm, tf), lambda t, j, te, nt: (t, j)),
            scratch_shapes=[pltpu.VMEM((tm, D), BF16)]),
        out_shape=jax.ShapeDtypeStruct((R, F), BF16),
        compiler_params=_params(2), name="moe_up")(tile_expert, n_tiles, xs, wg, wu)
    return pl.pallas_call(
        _moe_down_kernel,
        grid_spec=pltpu.PrefetchScalarGridSpec(
            num_scalar_prefetch=2, grid=(NT, D // tn),
            in_specs=[pl.BlockSpec((tm, F), lambda t, j, te, nt: (tile(t, nt), 0)),
                      pl.BlockSpec((1, F, tn), wmap(D // tn - 1))],
            out_specs=pl.BlockSpec((tm, tn), lambda t, j, te, nt: (t, j))),
        out_shape=jax.ShapeDtypeStruct((R, D), F32),
        compiler_params=_params(2), name="moe_down")(tile_expert, n_tiles, hidden, wd)


def _moe_combine_kernel(pos_ref, ys_hbm, x_ref, wt_ref, pw_ref, g_ref, o_ref, buf_ref, sem, *, tr):
    i = pl.program_id(0)
    n_steps = pl.num_programs(0)

    def loops(step, slot):
        def copies(r):
            return [pltpu.make_async_copy(ys_hbm.at[pl.ds(pos_ref[(step * tr + r) * TOP_K + kk], 1)],
                                          buf_ref.at[slot, kk, pl.ds(r, 1)], sem.at[slot])
                    for kk in range(TOP_K)]
        return _row_copy_loops(tr, copies)

    @pl.when(i == 0)
    def _():
        loops(0, 0)[0]()

    for parity in range(2):
        @pl.when(i % 2 == parity)
        def _():
            @pl.when(i + 1 < n_steps)
            def _():
                loops(i + 1, 1 - parity)[0]()

            loops(i, parity)[1]()
            wt = wt_ref[...]
            f = wt[:, 2:3] * buf_ref[parity, 0] + wt[:, 3:4] * buf_ref[parity, 1]
            fn = f * lax.rsqrt(jnp.mean(f * f, axis=-1, keepdims=True) + NORM_EPS)
            o_ref[...] = x_ref[...] + g_ref[...] * (fn * pw_ref[...])


def _moe_combine(pos, ys, x, route, post_w, gate, tr=128):
    T, D = x.shape
    tr = min(tr, T)
    assert tr % DMA_UNROLL == 0
    vec = pl.BlockSpec((1, D), lambda i, p: (0, 0))
    return pl.pallas_call(
        functools.partial(_moe_combine_kernel, tr=tr),
        grid_spec=pltpu.PrefetchScalarGridSpec(
            num_scalar_prefetch=1, grid=(T // tr,),
            in_specs=[pl.BlockSpec(memory_space=pl.ANY),
                      pl.BlockSpec((tr, D), lambda i, p: (i, 0)),
                      pl.BlockSpec((tr, LANES), lambda i, p: (i, 0)),
                      vec, vec],
            out_specs=pl.BlockSpec((tr, D), lambda i, p: (i, 0)),
            scratch_shapes=[pltpu.VMEM((2, TOP_K, tr, D), F32), pltpu.SemaphoreType.DMA((2,))]),
        out_shape=jax.ShapeDtypeStruct((T, D), F32),
        compiler_params=_params(1), name="moe_combine",
    )(pos, ys, x, route, post_w.reshape(1, D), gate.reshape(1, D))


def _moe_layer(x, h_packed, route, wg, wu, wd, post_w, gate, tm=512):
    T = x.shape[0]
    E = wg.shape[0]
    tm = min(tm, T)
    A = T * TOP_K
    R = _round_up(A, tm) + E * tm
    expert = route[:, :TOP_K].astype(jnp.int32).reshape(A)
    onehot = (expert[:, None] == jnp.arange(E, dtype=jnp.int32)[None, :]).astype(jnp.int32)
    csum = jnp.cumsum(onehot, axis=0)
    counts = csum[-1]
    rank = jnp.sum((csum - 1) * onehot, axis=1)
    padded = (counts + tm - 1) // tm * tm
    ends = jnp.cumsum(padded)
    offsets = ends - padded
    pos = (jnp.sum(onehot * offsets[None, :], axis=1) + rank).astype(jnp.int32)
    src = jnp.zeros((R,), jnp.int32).at[pos].set(jnp.arange(A, dtype=jnp.int32) // TOP_K)
    n_rows = ends[-1:].astype(jnp.int32)
    n_tiles = n_rows // tm
    tile_start = jnp.arange(R // tm, dtype=jnp.int32) * tm
    tile_expert = jnp.sum((tile_start[:, None] >= ends[None, :]).astype(jnp.int32), axis=1)
    tile_expert = jnp.minimum(tile_expert, jnp.max(jnp.where(counts > 0, jnp.arange(E), 0))).astype(jnp.int32)
    xs = _gather_rows(h_packed, src, n_rows, R)
    ys = _moe_experts(xs, tile_expert, n_tiles, wg, wu, wd, tm)
    return _moe_combine(pos, ys, x, route, post_w, gate)


def kernel(x, c, ctx, c_ctx, w_ada, b_ada, pre_norm_mix, post_norm_mix, pre_norm_ffn, post_norm_ffn, w_in, lambda_q1, lambda_k1, lambda_q2, lambda_k2, da_subln, gm_norm_w, gm_norm_b, gm_w_s, gm_b_s, w_branch_attn, w_branch_gmlp, w_out, ffn_w_gate, ffn_w_up, ffn_w_down, moe_w_router, moe_b_router, moe_w_gate, moe_w_up, moe_w_down):
    B, S, D = x.shape
    assert B == 1, "the kernels process one sequence"
    L = w_ada.shape[0]
    hd = da_subln.shape[1]
    dk = lambda_q1.shape[1]
    assert hd == 2 * dk == LANES
    qk_w = w_branch_attn.shape[1]
    gm_w = gm_norm_w.shape[1]
    u_col, v_col = 3 * qk_w, 3 * qk_w + gm_w
    ga_col, gb_col = 3 * qk_w + 2 * gm_w, 3 * qk_w + 2 * gm_w + D
    assert w_in.shape[2] == gb_col + D
    qscale = dk ** -0.5 * LOG2E

    xl, xc = x[0], ctx[0]
    cc = jnp.zeros((8, D), F32).at[0].set(c[0]).at[1].set(c_ctx)
    mod = _ada(cc, w_ada, b_ada)
    mod_l = mod[:, 0].reshape(L, 6, D)
    mod_c = mod[:, 1].reshape(L, 6, D)
    tables = _rope_tables(S, hd)

    hl = _norm_call(xl, pre=(pre_norm_mix[0], mod_l[0, 0], mod_l[0, 1]))[0]
    hc = _norm_call(xc, pre=(pre_norm_mix[0], mod_c[0, 0], mod_c[0, 1]))[0]

    w_in_next = None
    for l in range(L):
        need_ctx = l < L - 1
        last = l == L - 1
        lambda_init = 0.8 - 0.6 * math.exp(-0.3 * l)
        lam = (jnp.exp(jnp.sum(lambda_q1[l] * lambda_k1[l])) - jnp.exp(jnp.sum(lambda_q2[l] * lambda_k2[l]))
               + lambda_init).reshape(1).astype(F32)
        w_pa, w_pb, w_o = (w[l].astype(BF16) for w in (w_branch_attn, w_branch_gmlp, w_out))
        w_i = w_in[l].astype(BF16) if w_in_next is None else w_in_next
        out_scale = 1.0 - lambda_init
        is_moe = l % 2 == 1
        i = l // 2

        proj_l, proj_c = _matmul(hl, w_i, BF16), _matmul(hc, w_i, BF16)
        qT_l, qT_c, k_all, vT_all, amax = _prep(proj_l, proj_c, qk_w, hd, qscale, tables)
        k_scale = _fp8_scale(jnp.max(amax[2]))
        k8_all = _split8(k_all, k_scale.reshape(1), hd)
        def rows_of(stack, idx, out_cols=None):
            rows, cols = math.prod(stack.shape[1:-1]), stack.shape[-1]
            return (stack.reshape(-1, cols), out_cols or cols, idx * rows, rows)

        if is_moe:
            E, _, F = moe_w_gate.shape[1:]
            side = [rows_of(moe_w_gate, i), rows_of(moe_w_up, i), rows_of(moe_w_down, i)]
        else:
            FF = ffn_w_gate.shape[2]
            FFp = _round_up(FF, 1024)
            side = [rows_of(ffn_w_gate, i, FFp), rows_of(ffn_w_up, i, FFp)]
        if not last:
            side.append(rows_of(w_in, l + 1))
        attn_l, casts = _attention(lam, qT_l, jnp.max(amax[0]), k8_all, k_scale, vT_all,
                                   da_subln[l], out_scale, side=side)
        w_in_next = None if last else casts[-1]
        gm_l = _spatial_gating(proj_l, u_col, v_col, gm_norm_w[l], gm_norm_b[l], gm_w_s[l], gm_b_s[l])
        ol = _matmul(_gated_merge(attn_l, gm_l, proj_l, ga_col, gb_col, w_pa, w_pb), w_o, BF16)
        if need_ctx:
            attn_c, _ = _attention(lam, qT_c, jnp.max(amax[1]), k8_all, k_scale, vT_all, da_subln[l], out_scale,
                                   key0=S, n_keys=xc.shape[0])
            gm_c = _spatial_gating(proj_c, u_col, v_col, gm_norm_w[l], gm_norm_b[l], gm_w_s[l], gm_b_s[l])
            oc = _matmul(_gated_merge(attn_c, gm_c, proj_c, ga_col, gb_col, w_pa, w_pb), w_o, BF16)

        def mixer_update(xr, o, m):
            router = (moe_w_router[i], moe_b_router[i]) if is_moe else None
            return _norm_call(xr, update=(o, post_norm_mix[l], m[l, 2]),
                              pre=(pre_norm_ffn[l], m[l, 3], m[l, 4]), router=router, pack_h=is_moe)

        def next_pre(m):
            return None if last else (pre_norm_mix[l + 1], m[l + 1, 0], m[l + 1, 1])

        if is_moe:
            wg, wu, wd = casts[0].reshape(E, D, F), casts[1].reshape(E, D, F), casts[2].reshape(E, F, D)
            xl, hl2, route = mixer_update(xl, ol, mod_l)
            xl = _moe_layer(xl, hl2, route, wg, wu, wd, post_norm_ffn[l], mod_l[l, 5])
            if not last:
                hl = _norm_call(xl, pre=next_pre(mod_l))[0]
            if need_ctx:
                xc, hc2, route_c = mixer_update(xc, oc, mod_c)
                xc = _moe_layer(xc, hc2, route_c, wg, wu, wd, post_norm_ffn[l], mod_c[l, 5])
                hc = _norm_call(xc, pre=next_pre(mod_c))[0]
        else:
            wg, wu = casts[0], casts[1]
            wd = jnp.pad(ffn_w_down[i].astype(BF16), ((0, FFp - FF), (0, 0)))

            def dense(xr, o, m):
                xr, h2 = mixer_update(xr, o, m)
                f = _matmul(_swiglu_up(h2, wg, wu), wd, BF16)
                res = _norm_call(xr, update=(f, post_norm_ffn[l], m[l, 5]), pre=next_pre(m))
                return res[0], (res[1] if not last else None)

            xl, hl = dense(xl, ol, mod_l)
            if need_ctx:
                xc, hc = dense(xc, oc, mod_c)
            elif not last:
                hc = _norm_call(xc, pre=next_pre(mod_c))[0]
    return xl[None]
```

```python
import functools
import math

import jax
import jax.numpy as jnp
from jax import lax
from jax.experimental import pallas as pl
from jax.experimental.pallas import tpu as pltpu

F32 = jnp.float32
BF16 = jnp.bfloat16
F8 = jnp.float8_e4m3fn

GRID_W = 64
ROPE_THETA = 10000.0
NORM_EPS = 1e-6
TOP_K = 2
LANES = 128
MXU_COLS = 256
ONES_ROWS = 16
VMEM_LIMIT = 56 * 1024 * 1024
LOG2E = 1.4426950408889634


def _params(n_axes):
    return pltpu.CompilerParams(dimension_semantics=("arbitrary",) * n_axes,
                                vmem_limit_bytes=VMEM_LIMIT)


def _dot(a, b):
    return jnp.dot(a, b, preferred_element_type=F32)


def _round_up(n, m):
    return (n + m - 1) // m * m


def _tile(n, pref):
    if n <= pref:
        return n
    t = pref - pref % LANES
    while n % t:
        t -= LANES
    return t


def _ada_kernel(c_ref, w_ref, b_ref, o_ref):
    c = c_ref[...]
    s = (c * jax.nn.sigmoid(c)).astype(BF16)
    o_ref[0] = _dot(s, w_ref[0].astype(BF16)) + b_ref[0]


def _ada(cc, w_ada, b_ada):
    L, D, N = w_ada.shape
    R = cc.shape[0]
    tn = _tile(N, 512)
    return pl.pallas_call(
        _ada_kernel,
        grid=(L, N // tn),
        in_specs=[pl.BlockSpec((R, D), lambda l, j: (0, 0)),
                  pl.BlockSpec((1, D, tn), lambda l, j: (l, 0, j)),
                  pl.BlockSpec((1, 1, tn), lambda l, j: (l, 0, j))],
        out_specs=pl.BlockSpec((1, R, tn), lambda l, j: (l, 0, j)),
        out_shape=jax.ShapeDtypeStruct((L, R, N), F32),
        compiler_params=_params(2),
        name="ada",
    )(cc, w_ada, b_ada.reshape(L, 1, N))


def _top2_rows(logits, n_experts):
    lane = lax.broadcasted_iota(jnp.int32, logits.shape, 1)
    lane_f = lane.astype(F32)
    neg = jnp.float32(-jnp.inf)
    lg = jnp.where(lane < n_experts, logits, neg)
    v1 = jnp.max(lg, axis=-1, keepdims=True)
    i1 = jnp.min(jnp.where(lg == v1, lane_f, float(LANES)), axis=-1, keepdims=True)
    lg2 = jnp.where(lane_f == i1, neg, lg)
    v2 = jnp.max(lg2, axis=-1, keepdims=True)
    i2 = jnp.min(jnp.where(lg2 == v2, lane_f, float(LANES)), axis=-1, keepdims=True)
    e = jnp.exp(v2 - v1)
    w1 = 1.0 / (1.0 + e)
    w2 = e / (1.0 + e)
    return jnp.where(lane == 0, i1, jnp.where(lane == 1, i2, jnp.where(lane == 2, w1, jnp.where(lane == 3, w2, 0.0))))


def _norm_kernel(*refs, has_update, has_pre, pack_h, n_experts):
    it = iter(refs)
    x_ref = next(it)
    if has_update:
        u_ref, pw_ref, g_ref = next(it), next(it), next(it)
    if has_pre:
        w_ref, sh_ref, sc_ref = next(it), next(it), next(it)
    if n_experts:
        wr_ref, br_ref = next(it), next(it)
    if has_update:
        xo_ref = next(it)
    if has_pre:
        h_ref = next(it)
    if n_experts:
        r_ref = next(it)

    x = x_ref[...]
    if has_update:
        u = u_ref[...].astype(F32)
        un = u * lax.rsqrt(jnp.mean(u * u, axis=-1, keepdims=True) + NORM_EPS)
        x = x + g_ref[...] * (un * pw_ref[...])
        xo_ref[...] = x
    if has_pre:
        y = x * lax.rsqrt(jnp.mean(x * x, axis=-1, keepdims=True) + NORM_EPS)
        h = (y * w_ref[...]) * (1.0 + sc_ref[...]) + sh_ref[...]
        if pack_h:
            half = h.shape[1] // 2
            lo = pltpu.bitcast(h[:, :half].astype(BF16).astype(F32), jnp.uint32)
            hi = pltpu.bitcast(h[:, half:].astype(BF16).astype(F32), jnp.uint32)
            h_ref[...] = (lo >> 16) | (hi & jnp.uint32(0xFFFF0000))
        else:
            h_ref[...] = h.astype(BF16)
        if n_experts:
            wr = wr_ref[...]
            h_hi = h.astype(BF16)
            h_lo = (h - h_hi.astype(F32)).astype(BF16)
            w_hi = wr.astype(BF16)
            w_lo = (wr - w_hi.astype(F32)).astype(BF16)
            logits = _dot(h_hi, w_hi) + (_dot(h_hi, w_lo) + _dot(h_lo, w_hi)) + br_ref[...]
            r_ref[...] = _top2_rows(logits, n_experts)


def _norm_call(x, update=None, pre=None, router=None, pack_h=False):
    T, D = x.shape
    tr = min(256, T)
    row = pl.BlockSpec((tr, D), lambda i: (i, 0))
    vec = pl.BlockSpec((1, D), lambda i: (0, 0))
    args, in_specs, out_shape, out_specs = [x], [row], [], []
    if update is not None:
        upd, pw, g = update
        args += [upd, pw.reshape(1, D), g.reshape(1, D)]
        in_specs += [row, vec, vec]
        out_shape.append(jax.ShapeDtypeStruct((T, D), F32))
        out_specs.append(row)
    if pre is not None:
        w, sh, sc = pre
        args += [w.reshape(1, D), sh.reshape(1, D), sc.reshape(1, D)]
        in_specs += [vec, vec, vec]
        if pack_h:
            out_shape.append(jax.ShapeDtypeStruct((T, D // 2), jnp.uint32))
            out_specs.append(pl.BlockSpec((tr, D // 2), lambda i: (i, 0)))
        else:
            out_shape.append(jax.ShapeDtypeStruct((T, D), BF16))
            out_specs.append(row)
    n_experts = 0
    if router is not None:
        wr, br = router
        n_experts = wr.shape[1]
        wr_p = jnp.zeros((D, LANES), F32).at[:, :n_experts].set(wr)
        br_p = jnp.zeros((1, LANES), F32).at[0, :n_experts].set(br)
        args += [wr_p, br_p]
        in_specs += [pl.BlockSpec((D, LANES), lambda i: (0, 0)), pl.BlockSpec((1, LANES), lambda i: (0, 0))]
        out_shape.append(jax.ShapeDtypeStruct((T, LANES), F32))
        out_specs.append(pl.BlockSpec((tr, LANES), lambda i: (i, 0)))
    kern = functools.partial(_norm_kernel, has_update=update is not None, has_pre=pre is not None,
                             pack_h=pack_h, n_experts=n_experts)
    return pl.pallas_call(kern, grid=(T // tr,), in_specs=in_specs, out_specs=out_specs,
                          out_shape=out_shape, compiler_params=_params(1), name="norm")(*args)


def _mm_kernel(a_ref, b_ref, o_ref):
    o_ref[...] = _dot(a_ref[...], b_ref[...]).astype(o_ref.dtype)


def _mm_acc_kernel(a_ref, b_ref, o_ref, acc_ref, *, nk):
    k = pl.program_id(2)
    p = _dot(a_ref[...], b_ref[...])

    @pl.when(k == 0)
    def _():
        acc_ref[...] = p

    @pl.when(k > 0)
    def _():
        acc_ref[...] += p

    @pl.when(k == nk - 1)
    def _():
        o_ref[...] = acc_ref[...].astype(o_ref.dtype)


def _k_tile(K, limit=4096):
    if K <= limit:
        return K
    tk = limit - limit % 256
    while K % tk:
        tk -= 256
    return tk


def _matmul(a, b, out_dtype, tm=1024, tn=1024):
    M, K = a.shape
    N = b.shape[1]
    tm, tn, tk = _tile(M, tm), _tile(N, tn), _k_tile(K)
    out_shape = jax.ShapeDtypeStruct((M, N), out_dtype)
    if tk == K:
        return pl.pallas_call(
            _mm_kernel, grid=(M // tm, N // tn),
            in_specs=[pl.BlockSpec((tm, K), lambda i, j: (i, 0)), pl.BlockSpec((K, tn), lambda i, j: (0, j))],
            out_specs=pl.BlockSpec((tm, tn), lambda i, j: (i, j)),
            out_shape=out_shape, compiler_params=_params(2), name="matmul")(a, b)
    nk = K // tk
    return pl.pallas_call(
        functools.partial(_mm_acc_kernel, nk=nk), grid=(M // tm, N // tn, nk),
        in_specs=[pl.BlockSpec((tm, tk), lambda i, j, k: (i, k)), pl.BlockSpec((tk, tn), lambda i, j, k: (k, j))],
        out_specs=pl.BlockSpec((tm, tn), lambda i, j, k: (i, j)),
        out_shape=out_shape, scratch_shapes=[pltpu.VMEM((tm, tn), F32)],
        compiler_params=_params(3), name="matmul_acc")(a, b)


def _swiglu_kernel(a_ref, wg_ref, wu_ref, o_ref):
    a = a_ref[...]
    g = _dot(a, wg_ref[...])
    u = _dot(a, wu_ref[...])
    o_ref[...] = (g * jax.nn.sigmoid(g) * u).astype(o_ref.dtype)


def _swiglu_up(a, wg, wu, tm=1024, tn=512):
    M, K = a.shape
    N = wg.shape[1]
    tm, tn = _tile(M, tm), _tile(N, tn)
    wspec = pl.BlockSpec((K, tn), lambda i, j: (0, j))
    return pl.pallas_call(
        _swiglu_kernel, grid=(M // tm, N // tn),
        in_specs=[pl.BlockSpec((tm, K), lambda i, j: (i, 0)), wspec, wspec],
        out_specs=pl.BlockSpec((tm, tn), lambda i, j: (i, j)),
        out_shape=jax.ShapeDtypeStruct((M, N), BF16), compiler_params=_params(2), name="swiglu_up")(a, wg, wu)


def _merge_kernel(a1_ref, a2_ref, w1_ref, w2_ref, g1_ref, g2_ref, o_ref):
    y1 = _dot(a1_ref[...], w1_ref[...])
    y2 = _dot(a2_ref[...], w2_ref[...])
    g1 = jax.nn.sigmoid(g1_ref[...].astype(F32))
    g2 = jax.nn.sigmoid(g2_ref[...].astype(F32))
    o_ref[...] = (g1 * y1 + g2 * y2).astype(o_ref.dtype)


def _gated_merge(attn, gm, proj, ga_col, gb_col, w_pa, w_pb, tm=1024, tn=512):
    M, K1 = attn.shape
    K2 = gm.shape[1]
    N = w_pa.shape[1]
    tm, tn = _tile(M, tm), _tile(N, tn)
    ja, jb = ga_col // tn, gb_col // tn
    return pl.pallas_call(
        _merge_kernel, grid=(M // tm, N // tn),
        in_specs=[pl.BlockSpec((tm, K1), lambda i, j: (i, 0)),
                  pl.BlockSpec((tm, K2), lambda i, j: (i, 0)),
                  pl.BlockSpec((K1, tn), lambda i, j: (0, j)),
                  pl.BlockSpec((K2, tn), lambda i, j: (0, j)),
                  pl.BlockSpec((tm, tn), lambda i, j: (i, ja + j)),
                  pl.BlockSpec((tm, tn), lambda i, j: (i, jb + j))],
        out_specs=pl.BlockSpec((tm, tn), lambda i, j: (i, j)),
        out_shape=jax.ShapeDtypeStruct((M, N), BF16), compiler_params=_params(2), name="gated_merge",
    )(attn, gm, w_pa, w_pb, proj, proj)


def _prep_kernel(ql_ref, kl_ref, vl_ref, cos_ref, sin_ref, qc_ref, kc_ref, vc_ref,
                 qTl_ref, qTc_ref, ko_ref, vT_ref, amax_ref, *, heads, hd, qscale, n_lat):
    i = pl.program_id(0)
    hv = hd + ONES_ROWS
    row = lax.broadcasted_iota(jnp.int32, (8, hd), 0)

    @pl.when(i == 0)
    def _():
        amax_ref[...] = jnp.zeros(amax_ref.shape, F32)

    def emit(q_ref, k_ref, v_ref, rot, qT_ref, q_row):
        ones = jnp.ones((ONES_ROWS, q_ref.shape[0]), BF16)
        qmax = kmax = None
        for h in range(heads):
            sl = slice(h * hd, (h + 1) * hd)
            q = (rot(q_ref[:, sl].astype(F32)) * qscale).astype(BF16)
            k = rot(k_ref[:, sl].astype(F32)).astype(BF16)
            qT_ref[sl, :] = q.astype(F32).T.astype(BF16)
            ko_ref[:, sl] = k
            vT_ref[h * hv:h * hv + hd, :] = v_ref[:, sl].astype(F32).T.astype(BF16)
            vT_ref[h * hv + hd:(h + 1) * hv, :] = ones
            qa, ka = jnp.abs(q.astype(F32)), jnp.abs(k.astype(F32))
            qmax = qa if qmax is None else jnp.maximum(qmax, qa)
            kmax = ka if kmax is None else jnp.maximum(kmax, ka)
        blk = jnp.where(row == q_row, jnp.max(qmax, axis=0, keepdims=True),
                        jnp.where(row == 2, jnp.max(kmax, axis=0, keepdims=True), 0.0))
        amax_ref[...] = jnp.maximum(amax_ref[...], blk)

    @pl.when(i < n_lat)
    def _():
        cos, sin = cos_ref[...], sin_ref[...]
        lane = lax.broadcasted_iota(jnp.int32, cos.shape, 1)
        first = (lane % 32) < 16

        def rot(t):
            partner = jnp.where(first, pltpu.roll(t, hd - 16, 1), pltpu.roll(t, 16, 1))
            return t * cos + partner * sin

        emit(ql_ref, kl_ref, vl_ref, rot, qTl_ref, 0)

    @pl.when(i >= n_lat)
    def _():
        emit(qc_ref, kc_ref, vc_ref, lambda t: t, qTc_ref, 1)


def _prep(proj_l, proj_c, width, hd, qscale, tables):
    S, C = proj_l.shape[0], proj_c.shape[0]
    tr = min(256, S, C)
    n_lat, n_ctx = S // tr, C // tr
    heads = width // hd
    lat = lambda i: jnp.minimum(i, n_lat - 1)
    ctx = lambda i: jnp.maximum(i - n_lat, 0)
    col_l = lambda c: pl.BlockSpec((tr, width), lambda i: (lat(i), c))
    col_c = lambda c: pl.BlockSpec((tr, width), lambda i: (ctx(i), c))
    tab = pl.BlockSpec((tr, hd), lambda i: (lat(i), 0))
    vrows = heads * (hd + ONES_ROWS)
    return pl.pallas_call(
        functools.partial(_prep_kernel, heads=heads, hd=hd, qscale=qscale, n_lat=n_lat),
        grid=(n_lat + n_ctx,),
        in_specs=[col_l(0), col_l(1), col_l(2), tab, tab, col_c(0), col_c(1), col_c(2)],
        out_specs=[pl.BlockSpec((width, tr), lambda i: (0, lat(i))), pl.BlockSpec((width, tr), lambda i: (0, ctx(i))),
                   pl.BlockSpec((tr, width), lambda i: (i, 0)), pl.BlockSpec((vrows, tr), lambda i: (0, i)),
                   pl.BlockSpec((8, hd), lambda i: (0, 0))],
        out_shape=[jax.ShapeDtypeStruct((width, S), BF16), jax.ShapeDtypeStruct((width, C), BF16),
                   jax.ShapeDtypeStruct((S + C, width), BF16), jax.ShapeDtypeStruct((vrows, S + C), BF16),
                   jax.ShapeDtypeStruct((8, hd), F32)],
        compiler_params=_params(1), name="prep",
    )(proj_l, proj_l, proj_l, tables[0], tables[1], proj_c, proj_c, proj_c)


def _split8_kernel(sc_ref, k_ref, o_ref, *, heads, hd):
    half = hd // 2
    lane = lax.broadcasted_iota(jnp.int32, (k_ref.shape[0], hd), 1)
    low = lane < half
    for h in range(heads):
        x = k_ref[:, h * hd:(h + 1) * hd].astype(F32) * sc_ref[0]
        hi = x.astype(F8).astype(F32)
        lo = (x - hi).astype(F8).astype(F32)
        hi_r, lo_r = pltpu.roll(hi, half, 1), pltpu.roll(lo, half, 1)
        pieces = (jnp.where(low, hi, lo_r), jnp.where(low, hi, 0.0),
                  jnp.where(low, hi_r, lo), jnp.where(low, hi_r, 0.0))
        for i, piece in enumerate(pieces):
            o_ref[:, (4 * h + i) * hd:(4 * h + i + 1) * hd] = piece.astype(F8)


def _split8(k, scale, hd):
    T, width = k.shape
    tr = min(256, T)
    return pl.pallas_call(
        functools.partial(_split8_kernel, heads=width // hd, hd=hd),
        grid=(T // tr,),
        in_specs=[pl.BlockSpec(memory_space=pltpu.SMEM), pl.BlockSpec((tr, width), lambda i: (i, 0))],
        out_specs=pl.BlockSpec((tr, 4 * width), lambda i: (i, 0)),
        out_shape=jax.ShapeDtypeStruct((T, 4 * width), F8),
        compiler_params=_params(1), name="split8")(scale, k)


def _fp8_scale(amax):
    e = jnp.floor(jnp.log2(256.0 / jnp.maximum(amax, 1e-30)))
    return jnp.exp2(jnp.clip(e, -60.0, 60.0))


def _rope_tables(n_tokens, hd):
    freqs = hd // 8
    lane = jnp.arange(hd)
    inv = 1.0 / (ROPE_THETA ** ((lane % freqs).astype(F32) / freqs))
    col_axis = (lane % (4 * freqs)) >= 2 * freqs
    sign = jnp.where(lane % (2 * freqs) < freqs, -1.0, 1.0)
    t = jnp.arange(n_tokens)[:, None]
    ang = jnp.where(col_axis, t % GRID_W, t // GRID_W).astype(F32) * inv
    return jnp.cos(ang), sign * jnp.sin(ang)


SIDE_STEPS = 8


def _attn_kernel(sc_ref, qT_ref, k_ref, vT_ref, sub_ref, *rest, n, tq, dk, out_scale, n_side):
    side_in, o_ref, side_out = rest[:n_side], rest[n_side], rest[n_side + 1:2 * n_side + 1]
    q8_ref, s_ref, mt_ref, m_ref, acc_ref = rest[2 * n_side + 1:]
    _attn_body(sc_ref, qT_ref, k_ref, vT_ref, sub_ref, o_ref, q8_ref, s_ref, mt_ref, m_ref, acc_ref,
               side_in, side_out, n=n, tq=tq, dk=dk, out_scale=out_scale)


def _attn_body(sc_ref, qT_ref, k_ref, vT_ref, sub_ref, o_ref, q8_ref, s_ref, mt_ref, m_ref, acc_ref,
               side_in, side_out, *, n, tq, dk, out_scale):
    j = pl.program_id(2)
    W = 2 * tq
    dv = 2 * dk
    kw = 4 * dk
    n_groups = W // MXU_COLS
    per_map = tq // MXU_COLS

    def scores(slot, c):
        cols = slice(c * MXU_COLS, (c + 1) * MXU_COLS)
        mp, qc = c // per_map, c % per_map
        s = _dot(k_ref[:, mp * kw:(mp + 1) * kw],
                 q8_ref[mp, :, qc * MXU_COLS:(qc + 1) * MXU_COLS]).astype(BF16)
        s_ref[slot, :, cols] = s
        mt_ref[slot, :, cols] = jnp.max(s, axis=0, keepdims=True).astype(F32)

    def consume(slot, c):
        cols = slice(c * MXU_COLS, (c + 1) * MXU_COLS)
        unscale = sc_ref[2]
        m_prev = m_ref[:, cols]
        m_new = jnp.maximum(m_prev, mt_ref[slot, :, cols])
        alpha = jnp.exp2((m_prev - m_new) * unscale)
        p = jnp.exp2((s_ref[slot, :, cols] - m_new.astype(BF16)) * unscale.astype(BF16))
        acc_ref[:, cols] = alpha * acc_ref[:, cols] + _dot(vT_ref[...], p)
        m_ref[:, cols] = m_new

    @pl.when(j == 0)
    def _first():
        q = qT_ref[...].astype(F32) * sc_ref[1]
        hi = q.astype(F8)
        lo = (q - hi.astype(F32)).astype(F8)
        for mp in range(2):
            rows = slice(mp * dk, (mp + 1) * dk)
            q8_ref[mp, 0:dk, :] = hi[rows]
            q8_ref[mp, dk:2 * dk, :] = hi[rows]
            q8_ref[mp, 2 * dk:3 * dk, :] = lo[rows]
            q8_ref[mp, 3 * dk:kw, :] = jnp.zeros((dk, tq), F8)
        m_ref[...] = jnp.full(m_ref.shape, -jnp.inf, F32)
        acc_ref[...] = jnp.zeros(acc_ref.shape, F32)
        for c in range(n_groups):
            scores(0, c)

    for parity in range(2):
        @pl.when((j > 0) & (j < n) & (j % 2 == parity))
        def _steady():
            for c in range(n_groups):
                scores(parity, c)
                consume(1 - parity, c)
            for a_ref, b_ref in zip(side_in, side_out):
                cols = a_ref.shape[1]
                b_ref[:, 0:cols] = a_ref[...].astype(b_ref.dtype)
                if b_ref.shape[1] > cols:
                    b_ref[:, cols:] = jnp.zeros((b_ref.shape[0], b_ref.shape[1] - cols), b_ref.dtype)

    @pl.when(j == n)
    def _last():
        for c in range(n_groups):
            consume((n - 1) % 2, c)
        o_both = acc_ref[0:dv, :] / acc_ref[dv:dv + 1, :]
        o = o_both[:, 0:tq] - sc_ref[0] * o_both[:, tq:W]
        y = o * lax.rsqrt(jnp.mean(o * o, axis=0, keepdims=True) + NORM_EPS)
        y = y * (sub_ref[...] * out_scale)
        o_ref[...] = y.T.astype(o_ref.dtype)


def _attention(lam, qT, qmax, k8, k_scale, vT, subln, out_scale, key0=0, n_keys=None, side=(), tq=2048, tk=1280):
    W, Sq = qT.shape
    hd = subln.shape[0]
    hv = hd + ONES_ROWS
    H = W // hd
    T = n_keys or k8.shape[0]
    tq, tk = _tile(Sq, tq), _tile(T, tk)
    n = T // tk
    nq = Sq // tq
    assert key0 % tk == 0
    kb0 = key0 // tk
    q_scale = _fp8_scale(qmax)
    scalars = jnp.stack([lam.reshape(()), q_scale, 1.0 / (q_scale * k_scale)]).astype(F32)
    plain_casts = []
    if n - 1 < SIDE_STEPS:
        plain_casts = [jnp.pad(w[r0:r0 + nr].astype(BF16), ((0, 0), (0, oc - w.shape[1]))) for w, oc, r0, nr in side]
        side = ()
    total = H * nq * SIDE_STEPS

    def chunk_step(h, i, j):
        return (h * nq + i) * SIDE_STEPS + jnp.clip(j - 1, 0, SIDE_STEPS - 1)

    side_in_specs, side_out_specs, side_shapes = [], [], []
    for w, out_cols, row0, rows in side:
        cols = w.shape[1]
        chunks = total
        while chunks > 1 and (total % chunks or rows % chunks or (rows // chunks) % 16):
            chunks -= 1
        assert rows % chunks == 0 and total % chunks == 0 and row0 % (rows // chunks) == 0
        stride, blk0 = total // chunks, row0 // (rows // chunks)
        omap = lambda h, i, j, stride=stride: (chunk_step(h, i, j) // stride, 0)
        imap = lambda h, i, j, stride=stride, blk0=blk0: (blk0 + chunk_step(h, i, j) // stride, 0)
        side_in_specs.append(pl.BlockSpec((rows // chunks, cols), imap))
        side_out_specs.append(pl.BlockSpec((rows // chunks, out_cols), omap))
        side_shapes.append(jax.ShapeDtypeStruct((rows, out_cols), BF16))
    outs = pl.pallas_call(
        functools.partial(_attn_kernel, n=n, tq=tq, dk=hd // 2, out_scale=out_scale, n_side=len(side)),
        grid=(H, nq, n + 1),
        in_specs=[pl.BlockSpec(memory_space=pltpu.SMEM),
                  pl.BlockSpec((hd, tq), lambda h, i, j: (h, i)),
                  pl.BlockSpec((tk, 4 * hd), lambda h, i, j: (kb0 + jnp.minimum(j, n - 1), h)),
                  pl.BlockSpec((hv, tk), lambda h, i, j: (h, kb0 + jnp.maximum(j - 1, 0))),
                  pl.BlockSpec((hd, 1), lambda h, i, j: (0, 0))] + side_in_specs,
        out_specs=[pl.BlockSpec((tq, hd), lambda h, i, j: (i, h))] + side_out_specs,
        out_shape=[jax.ShapeDtypeStruct((Sq, W), BF16)] + side_shapes,
        scratch_shapes=[pltpu.VMEM((2, 2 * hd, tq), F8), pltpu.VMEM((2, tk, 2 * tq), BF16),
                        pltpu.VMEM((2, 1, 2 * tq), F32), pltpu.VMEM((1, 2 * tq), F32),
                        pltpu.VMEM((hv, 2 * tq), F32)],
        compiler_params=_params(3), name="diff_attention",
    )(scalars, qT, k8, vT, subln.reshape(hd, 1), *[entry[0] for entry in side])
    return outs[0], list(outs[1:]) + plain_casts


def _gmlp_kernel(u_ref, v_ref, gw_ref, gb_ref, ws_ref, bs_ref, o_ref, *, groups, chunk, gdim):
    v = v_ref[...].astype(F32)
    vc = v - jnp.mean(v, axis=-1, keepdims=True)
    vn = vc * lax.rsqrt(jnp.mean(vc * vc, axis=-1, keepdims=True) + NORM_EPS)
    vn = (vn * gw_ref[...] + gb_ref[...]).astype(BF16)
    n_chunks = v.shape[0] // chunk
    for g in range(groups):
        cols = slice(g * gdim, (g + 1) * gdim)
        rhs = jnp.concatenate([vn[c * chunk:(c + 1) * chunk, cols] for c in range(n_chunks)], axis=1)
        mixed = _dot(ws_ref[g], rhs)
        for c in range(n_chunks):
            rows = slice(c * chunk, (c + 1) * chunk)
            m = mixed[:, c * gdim:(c + 1) * gdim] + bs_ref[g]
            o_ref[rows, cols] = (u_ref[rows, cols].astype(F32) * m).astype(o_ref.dtype)


def _spatial_gating(proj, u_col, v_col, gn_w, gn_b, w_s, b_s):
    T = proj.shape[0]
    G, chunk, _ = w_s.shape
    GW = gn_w.shape[0]
    gdim = GW // G
    tr = min(512, T)
    bias = jnp.broadcast_to(b_s[:, :, None], (G, chunk, gdim)).astype(F32)
    vec = pl.BlockSpec((1, GW), lambda i: (0, 0))
    return pl.pallas_call(
        functools.partial(_gmlp_kernel, groups=G, chunk=chunk, gdim=gdim),
        grid=(T // tr,),
        in_specs=[pl.BlockSpec((tr, GW), lambda i: (i, u_col // GW)),
                  pl.BlockSpec((tr, GW), lambda i: (i, v_col // GW)),
                  vec, vec,
                  pl.BlockSpec((G, chunk, chunk), lambda i: (0, 0, 0)),
                  pl.BlockSpec((G, chunk, gdim), lambda i: (0, 0, 0))],
        out_specs=pl.BlockSpec((tr, GW), lambda i: (i, 0)),
        out_shape=jax.ShapeDtypeStruct((T, GW), BF16),
        compiler_params=_params(1), name="spatial_gating",
    )(proj, proj, gn_w.reshape(1, GW), gn_b.reshape(1, GW), w_s.astype(BF16), bias)


DMA_UNROLL = 8


def _row_copy_loops(n_rows, copies):
    def start_all():
        def body(t, carry):
            for u in range(DMA_UNROLL):
                for c, d in enumerate(copies(t * DMA_UNROLL + u)):
                    d.start(priority=(u + c) % 2)
            return carry
        lax.fori_loop(0, n_rows // DMA_UNROLL, body, 0)

    def wait_all():
        def body(t, carry):
            for u in range(DMA_UNROLL):
                for d in copies(t * DMA_UNROLL + u):
                    d.wait()
            return carry
        lax.fori_loop(0, n_rows // DMA_UNROLL, body, 0)

    return start_all, wait_all


def _gather_rows_kernel(src_ref, nrows_ref, h_hbm, o_ref, buf_ref, sem, *, tg):
    i = pl.program_id(0)
    n_steps = pl.num_programs(0)

    def loops(step, slot):
        def copies(r):
            return [pltpu.make_async_copy(h_hbm.at[pl.ds(src_ref[step * tg + r], 1)],
                                          buf_ref.at[slot, pl.ds(r, 1)], sem.at[slot])]
        return _row_copy_loops(tg, copies)

    def used(step):
        return step * tg < nrows_ref[0]

    @pl.when((i == 0) & used(0))
    def _():
        loops(0, 0)[0]()

    for parity in range(2):
        @pl.when(i % 2 == parity)
        def _():
            @pl.when((i + 1 < n_steps) & used(i + 1))
            def _():
                loops(i + 1, 1 - parity)[0]()

            @pl.when(used(i))
            def _():
                loops(i, parity)[1]()
                o_ref[...] = buf_ref[parity]

            @pl.when(jnp.logical_not(used(i)))
            def _():
                o_ref[...] = jnp.zeros(o_ref.shape, o_ref.dtype)


def _gather_rows(h, src, n_rows, R, tg=256):
    W = h.shape[1]
    assert tg % DMA_UNROLL == 0
    return pl.pallas_call(
        functools.partial(_gather_rows_kernel, tg=tg),
        grid_spec=pltpu.PrefetchScalarGridSpec(
            num_scalar_prefetch=2, grid=(R // tg,),
            in_specs=[pl.BlockSpec(memory_space=pl.ANY)],
            out_specs=pl.BlockSpec((tg, W), lambda i, src, n: (i, 0)),
            scratch_shapes=[pltpu.VMEM((2, tg, W), h.dtype), pltpu.SemaphoreType.DMA((2,))]),
        out_shape=jax.ShapeDtypeStruct((R, W), h.dtype),
        compiler_params=_params(1), name="moe_gather")(src, n_rows, h)


def _unpack_bf16_pairs(w):
    lo = pltpu.bitcast(w << 16, F32).astype(BF16)
    hi = pltpu.bitcast(w & jnp.uint32(0xFFFF0000), F32).astype(BF16)
    return jnp.concatenate([lo, hi], axis=1)


def _moe_up_kernel(te_ref, nt_ref, x_ref, wg_ref, wu_ref, o_ref, xb_ref):
    t, j = pl.program_id(0), pl.program_id(1)

    @pl.when(t < nt_ref[0])
    def _():
        @pl.when(j == 0)
        def _():
            xb_ref[...] = _unpack_bf16_pairs(x_ref[...])

        a = xb_ref[...]
        g = _dot(a, wg_ref[0])
        u = _dot(a, wu_ref[0])
        o_ref[...] = (g * jax.nn.sigmoid(g) * u).astype(o_ref.dtype)

    @pl.when(t >= nt_ref[0])
    def _():
        o_ref[...] = jnp.zeros(o_ref.shape, o_ref.dtype)


def _moe_down_kernel(te_ref, nt_ref, a_ref, w_ref, o_ref):
    @pl.when(pl.program_id(0) < nt_ref[0])
    def _():
        o_ref[...] = _dot(a_ref[...], w_ref[0]).astype(o_ref.dtype)

    @pl.when(pl.program_id(0) >= nt_ref[0])
    def _():
        o_ref[...] = jnp.zeros(o_ref.shape, o_ref.dtype)


def _moe_experts(xs, tile_expert, n_tiles, wg, wu, wd, tm, tf=1024, tn=2048):
    R = xs.shape[0]
    E, D, F = wg.shape
    NT = R // tm
    tf, tn = _tile(F, tf), _tile(D, tn)
    tile = lambda t, nt: jnp.minimum(t, nt[0] - 1)

    def wmap(last_j):
        return lambda t, j, te, nt: (te[t], 0, jnp.where(t < nt[0], j, last_j))

    hidden = pl.pallas_call(
        _moe_up_kernel,
        grid_spec=pltpu.PrefetchScalarGridSpec(
            num_scalar_prefetch=2, grid=(NT, F // tf),
            in_specs=[pl.BlockSpec((tm, D // 2), lambda t, j, te, nt: (tile(t, nt), 0)),
                      pl.BlockSpec((1, D, tf), wmap(F // tf - 1)),
                      pl.BlockSpec((1, D, tf), wmap(F // tf - 1))],
            out_specs=pl.BlockSpec((tm, tf), lambda t, j, te, nt: (t, j)),
            scratch_shapes=[pltpu.VMEM((tm, D), BF16)]),
        out_shape=jax.ShapeDtypeStruct((R, F), BF16),
        compiler_params=_params(2), name="moe_up")(tile_expert, n_tiles, xs, wg, wu)
    return pl.pallas_call(
        _moe_down_kernel,
        grid_spec=pltpu.PrefetchScalarGridSpec(
            num_scalar_prefetch=2, grid=(NT, D // tn),
            in_specs=[pl.BlockSpec((tm, F), lambda t, j, te, nt: (tile(t, nt), 0)),
                      pl.BlockSpec((1, F, tn), wmap(D // tn - 1))],
            out_specs=pl.BlockSpec((tm, tn), lambda t, j, te, nt: (t, j))),
        out_shape=jax.ShapeDtypeStruct((R, D), F32),
        compiler_params=_params(2), name="moe_down")(tile_expert, n_tiles, hidden, wd)


def _moe_combine_kernel(pos_ref, ys_hbm, x_ref, wt_ref, pw_ref, g_ref, o_ref, buf_ref, sem, *, tr):
    i = pl.program_id(0)
    n_steps = pl.num_programs(0)

    def loops(step, slot):
        def copies(r):
            return [pltpu.make_async_copy(ys_hbm.at[pl.ds(pos_ref[(step * tr + r) * TOP_K + kk], 1)],
                                          buf_ref.at[slot, kk, pl.ds(r, 1)], sem.at[slot])
                    for kk in range(TOP_K)]
        return _row_copy_loops(tr, copies)

    @pl.when(i == 0)
    def _():
        loops(0, 0)[0]()

    for parity in range(2):
        @pl.when(i % 2 == parity)
        def _():
            @pl.when(i + 1 < n_steps)
            def _():
                loops(i + 1, 1 - parity)[0]()

            loops(i, parity)[1]()
            wt = wt_ref[...]
            f = wt[:, 2:3] * buf_ref[parity, 0] + wt[:, 3:4] * buf_ref[parity, 1]
            fn = f * lax.rsqrt(jnp.mean(f * f, axis=-1, keepdims=True) + NORM_EPS)
            o_ref[...] = x_ref[...] + g_ref[...] * (fn * pw_ref[...])


def _moe_combine(pos, ys, x, route, post_w, gate, tr=128):
    T, D = x.shape
    tr = min(tr, T)
    assert tr % DMA_UNROLL == 0
    vec = pl.BlockSpec((1, D), lambda i, p: (0, 0))
    return pl.pallas_call(
        functools.partial(_moe_combine_kernel, tr=tr),
        grid_spec=pltpu.PrefetchScalarGridSpec(
            num_scalar_prefetch=1, grid=(T // tr,),
            in_specs=[pl.BlockSpec(memory_space=pl.ANY),
                      pl.BlockSpec((tr, D), lambda i, p: (i, 0)),
                      pl.BlockSpec((tr, LANES), lambda i, p: (i, 0)),
                      vec, vec],
            out_specs=pl.BlockSpec((tr, D), lambda i, p: (i, 0)),
            scratch_shapes=[pltpu.VMEM((2, TOP_K, tr, D), F32), pltpu.SemaphoreType.DMA((2,))]),
        out_shape=jax.ShapeDtypeStruct((T, D), F32),
        compiler_params=_params(1), name="moe_combine",
    )(pos, ys, x, route, post_w.reshape(1, D), gate.reshape(1, D))


def _moe_layer(x, h_packed, route, wg, wu, wd, post_w, gate, tm=512):
    T = x.shape[0]
    E = wg.shape[0]
    tm = min(tm, T)
    A = T * TOP_K
    R = _round_up(A, tm) + E * tm
    expert = route[:, :TOP_K].astype(jnp.int32).reshape(A)
    onehot = (expert[:, None] == jnp.arange(E, dtype=jnp.int32)[None, :]).astype(jnp.int32)
    csum = jnp.cumsum(onehot, axis=0)
    counts = csum[-1]
    rank = jnp.sum((csum - 1) * onehot, axis=1)
    padded = (counts + tm - 1) // tm * tm
    ends = jnp.cumsum(padded)
    offsets = ends - padded
    pos = (jnp.sum(onehot * offsets[None, :], axis=1) + rank).astype(jnp.int32)
    src = jnp.zeros((R,), jnp.int32).at[pos].set(jnp.arange(A, dtype=jnp.int32) // TOP_K)
    n_rows = ends[-1:].astype(jnp.int32)
    n_tiles = n_rows // tm
    tile_start = jnp.arange(R // tm, dtype=jnp.int32) * tm
    tile_expert = jnp.sum((tile_start[:, None] >= ends[None, :]).astype(jnp.int32), axis=1)
    tile_expert = jnp.minimum(tile_expert, jnp.max(jnp.where(counts > 0, jnp.arange(E), 0))).astype(jnp.int32)
    xs = _gather_rows(h_packed, src, n_rows, R)
    ys = _moe_experts(xs, tile_expert, n_tiles, wg, wu, wd, tm)
    return _moe_combine(pos, ys, x, route, post_w, gate)


def kernel(x, c, ctx, c_ctx, w_ada, b_ada, pre_norm_mix, post_norm_mix, pre_norm_ffn, post_norm_ffn, w_in, lambda_q1, lambda_k1, lambda_q2, lambda_k2, da_subln, gm_norm_w, gm_norm_b, gm_w_s, gm_b_s, w_branch_attn, w_branch_gmlp, w_out, ffn_w_gate, ffn_w_up, ffn_w_down, moe_w_router, moe_b_router, moe_w_gate, moe_w_up, moe_w_down):
    B, S, D = x.shape
    assert B == 1, "the kernels process one sequence"
    L = w_ada.shape[0]
    hd = da_subln.shape[1]
    dk = lambda_q1.shape[1]
    assert hd == 2 * dk == LANES
    qk_w = w_branch_attn.shape[1]
    gm_w = gm_norm_w.shape[1]
    u_col, v_col = 3 * qk_w, 3 * qk_w + gm_w
    ga_col, gb_col = 3 * qk_w + 2 * gm_w, 3 * qk_w + 2 * gm_w + D
    assert w_in.shape[2] == gb_col + D
    qscale = dk ** -0.5 * LOG2E

    xl, xc = x[0], ctx[0]
    cc = jnp.zeros((8, D), F32).at[0].set(c[0]).at[1].set(c_ctx)
    mod = _ada(cc, w_ada, b_ada)
    mod_l = mod[:, 0].reshape(L, 6, D)
    mod_c = mod[:, 1].reshape(L, 6, D)
    tables = _rope_tables(S, hd)

    hl = _norm_call(xl, pre=(pre_norm_mix[0], mod_l[0, 0], mod_l[0, 1]))[0]
    hc = _norm_call(xc, pre=(pre_norm_mix[0], mod_c[0, 0], mod_c[0, 1]))[0]

    w_in_next = None
    for l in range(L):
        need_ctx = l < L - 1
        last = l == L - 1
        lambda_init = 0.8 - 0.6 * math.exp(-0.3 * l)
        lam = (jnp.exp(jnp.sum(lambda_q1[l] * lambda_k1[l])) - jnp.exp(jnp.sum(lambda_q2[l] * lambda_k2[l]))
               + lambda_init).reshape(1).astype(F32)
        w_pa, w_pb, w_o = (w[l].astype(BF16) for w in (w_branch_attn, w_branch_gmlp, w_out))
        w_i = w_in[l].astype(BF16) if w_in_next is None else w_in_next
        out_scale = 1.0 - lambda_init
        is_moe = l % 2 == 1
        i = l // 2

        proj_l, proj_c = _matmul(hl, w_i, BF16), _matmul(hc, w_i, BF16)
        qT_l, qT_c, k_all, vT_all, amax = _prep(proj_l, proj_c, qk_w, hd, qscale, tables)
        k_scale = _fp8_scale(jnp.max(amax[2]))
        k8_all = _split8(k_all, k_scale.reshape(1), hd)
        def rows_of(stack, idx, out_cols=None):
            rows, cols = math.prod(stack.shape[1:-1]), stack.shape[-1]
            return (stack.reshape(-1, cols), out_cols or cols, idx * rows, rows)

        if is_moe:
            E, _, F = moe_w_gate.shape[1:]
            side = [rows_of(moe_w_gate, i), rows_of(moe_w_up, i), rows_of(moe_w_down, i)]
        else:
            FF = ffn_w_gate.shape[2]
            FFp = _round_up(FF, 1024)
            side = [rows_of(ffn_w_gate, i, FFp), rows_of(ffn_w_up, i, FFp)]
        if not last:
            side.append(rows_of(w_in, l + 1))
        attn_l, casts = _attention(lam, qT_l, jnp.max(amax[0]), k8_all, k_scale, vT_all,
                                   da_subln[l], out_scale, side=side)
        w_in_next = None if last else casts[-1]
        gm_l = _spatial_gating(proj_l, u_col, v_col, gm_norm_w[l], gm_norm_b[l], gm_w_s[l], gm_b_s[l])
        ol = _matmul(_gated_merge(attn_l, gm_l, proj_l, ga_col, gb_col, w_pa, w_pb), w_o, BF16)
        if need_ctx:
            attn_c, _ = _attention(lam, qT_c, jnp.max(amax[1]), k8_all, k_scale, vT_all, da_subln[l], out_scale,
                                   key0=S, n_keys=xc.shape[0])
            gm_c = _spatial_gating(proj_c, u_col, v_col, gm_norm_w[l], gm_norm_b[l], gm_w_s[l], gm_b_s[l])
            oc = _matmul(_gated_merge(attn_c, gm_c, proj_c, ga_col, gb_col, w_pa, w_pb), w_o, BF16)

        def mixer_update(xr, o, m):
            router = (moe_w_router[i], moe_b_router[i]) if is_moe else None
            return _norm_call(xr, update=(o, post_norm_mix[l], m[l, 2]),
                              pre=(pre_norm_ffn[l], m[l, 3], m[l, 4]), router=router, pack_h=is_moe)

        def next_pre(m):
            return None if last else (pre_norm_mix[l + 1], m[l + 1, 0], m[l + 1, 1])

        if is_moe:
            wg, wu, wd = casts[0].reshape(E, D, F), casts[1].reshape(E, D, F), casts[2].reshape(E, F, D)
            xl, hl2, route = mixer_update(xl, ol, mod_l)
            xl = _moe_layer(xl, hl2, route, wg, wu, wd, post_norm_ffn[l], mod_l[l, 5])
            if not last:
                hl = _norm_call(xl, pre=next_pre(mod_l))[0]
            if need_ctx:
                xc, hc2, route_c = mixer_update(xc, oc, mod_c)
                xc = _moe_layer(xc, hc2, route_c, wg, wu, wd, post_norm_ffn[l], mod_c[l, 5])
                hc = _norm_call(xc, pre=next_pre(mod_c))[0]
        else:
            wg, wu = casts[0], casts[1]
            wd = jnp.pad(ffn_w_down[i].astype(BF16), ((0, FFp - FF), (0, 0)))

            def dense(xr, o, m):
                xr, h2 = mixer_update(xr, o, m)
                f = _matmul(_swiglu_up(h2, wg, wu), wd, BF16)
                res = _norm_call(xr, update=(f, post_norm_ffn[l], m[l, 5]), pre=next_pre(m))
                return res[0], (res[1] if not last else None)

            xl, hl = dense(xl, ol, mod_l)
            if need_ctx:
                xc, hc = dense(xc, oc, mod_c)
            elif not last:
                hc = _norm_call(xc, pre=next_pre(mod_c))[0]
    return xl[None]
```
